```python
import math
import jax, jax.numpy as jnp
from jax import lax
import numpy as np

D_MODEL = 2048
BATCH = 1
SEQ = 8192
DEPTH = 2

N_BRANCH = 4
BRANCH_WIDTH = D_MODEL // 4
SC_TAPS = 3
CONF_TAPS = 31
POOL_WINDOWS = (2, 4, 8, 16)
POOL_GROUP = BRANCH_WIDTH // len(POOL_WINDOWS)
SB_HEADS = 8
SB_HEAD_DIM = BRANCH_WIDTH // SB_HEADS
Q_BLOCK = 128
D_FF = 5632
N_IN_SLOTS = 9
EPS = 1e-6

kernel_name = "hybrid_gated_conv_pool_stickbreaking_block"


def rms_norm(x, g):
    xf = x.astype(jnp.float32)
    y = xf * lax.rsqrt(jnp.mean(xf * xf, axis=-1, keepdims=True) + EPS)
    return (y * g.astype(jnp.float32)).astype(x.dtype)


def layer_norm(x, g, b):
    xf = x.astype(jnp.float32)
    mu = jnp.mean(xf, axis=-1, keepdims=True)
    xc = xf - mu
    y = xc * lax.rsqrt(jnp.mean(xc * xc, axis=-1, keepdims=True) + EPS)
    return (y * g.astype(jnp.float32) + b.astype(jnp.float32)).astype(x.dtype)


def swiglu(h, w1, w3, w2):
    return (jax.nn.silu(h @ w1) * (h @ w3)) @ w2


def causal_depthwise_conv(u, w):
    taps, c = w.shape
    return lax.conv_general_dilated(
        u, w[:, None, :].astype(u.dtype), window_strides=(1,), padding=[(taps - 1, 0)],
        dimension_numbers=("NWC", "WIO", "NWC"), feature_group_count=c)


def multiscale_pool(u):
    b, s, c = u.shape
    uf = u.astype(jnp.float32)
    csum = jnp.cumsum(uf, axis=1)
    csp = jnp.concatenate([jnp.zeros((b, 1, c), jnp.float32), csum], axis=1)
    pos = jnp.arange(s, dtype=jnp.int32)
    outs = []
    for g, w in enumerate(POOL_WINDOWS):
        sl = slice(g * POOL_GROUP, (g + 1) * POOL_GROUP)
        lo = jnp.maximum(pos + 1 - w, 0)
        cnt = jnp.minimum(pos + 1, w).astype(jnp.float32)[None, :, None]
        win = csum[:, :, sl] - csp[:, lo, sl]
        outs.append(win / cnt - uf[:, :, sl])
    return jnp.concatenate(outs, axis=-1).astype(u.dtype)


def stick_breaking_attention(q, k, v):
    b, s, h, hd = q.shape
    nb = s // Q_BLOCK
    kt = k.transpose(0, 2, 1, 3)
    vt = v.transpose(0, 2, 1, 3)
    qb = q.reshape(b, nb, Q_BLOCK, h, hd).transpose(1, 0, 3, 2, 4)
    starts = jnp.arange(nb, dtype=jnp.int32) * Q_BLOCK
    key_pos = jnp.arange(s, dtype=jnp.int32)
    scale = 1.0 / math.sqrt(hd)

    def block(args):
        q_blk, start = args
        z = jnp.einsum("bhqd,bhkd->bhqk", q_blk, kt,
                       preferred_element_type=jnp.float32) * scale
        qpos = start + jnp.arange(Q_BLOCK, dtype=jnp.int32)
        mask = key_pos[None, :] < qpos[:, None]
        log_1mb = jnp.where(mask, jax.nn.log_sigmoid(-z), 0.0)
        between = lax.cumsum(log_1mb, axis=3, reverse=True) - log_1mb
        a = jnp.where(mask, jnp.exp(jax.nn.log_sigmoid(z) + between), 0.0)
        return jnp.einsum("bhqk,bhkd->bhqd", a.astype(vt.dtype), vt)

    out = lax.map(block, (qb, starts))
    return out.transpose(1, 0, 3, 2, 4).reshape(b, s, h * hd)


def setup_inputs(seed: int = 0) -> dict:
    key = jax.random.key(seed)
    ks = jax.random.split(key, 24)
    L, D, W, F, G, HD = DEPTH, D_MODEL, BRANCH_WIDTH, D_FF, POOL_GROUP, SB_HEAD_DIM

    def nrm(k, shape, fan_in):
        return jax.random.normal(k, shape, jnp.float32) * (fan_in ** -0.5)

    def gain(k, shape):
        return 1.0 + 0.05 * jax.random.normal(k, shape, jnp.float32)

    return {
        "x": jax.random.normal(ks[0], (BATCH, SEQ, D), jnp.float32),
        "ffn1_norm": gain(ks[1], (L, D)),
        "ffn1_w1": nrm(ks[2], (L, D, F), D),
        "ffn1_w3": nrm(ks[3], (L, D, F), D),
        "ffn1_w2": nrm(ks[4], (L, F, D), F),
        "mix_norm": gain(ks[5], (L, D)),
        "w_in": nrm(ks[6], (L, D, N_IN_SLOTS * W), D),
        "conv_a": nrm(ks[7], (L, SC_TAPS, W), SC_TAPS),
        "conv_b": nrm(ks[8], (L, CONF_TAPS, W), CONF_TAPS),
        "ln_b_gain": gain(ks[9], (L, W)),
        "ln_b_bias": 0.02 * jax.random.normal(ks[10], (L, W), jnp.float32),
        "pool_map": nrm(ks[11], (L, len(POOL_WINDOWS), G, G), G),
        "pool_scale": gain(ks[12], (L, W)),
        "q_norm": gain(ks[13], (L, HD)),
        "k_norm": gain(ks[14], (L, HD)),
        "w_branch": nrm(ks[15], (L, N_BRANCH, W, D), W),
        "w_gate": nrm(ks[16], (L, D, N_BRANCH * D), D),
        "w_out": nrm(ks[17], (L, D, D), D),
        "ffn2_norm": gain(ks[18], (L, D)),
        "ffn2_w1": nrm(ks[19], (L, D, F), D),
        "ffn2_w3": nrm(ks[20], (L, D, F), D),
        "ffn2_w2": nrm(ks[21], (L, F, D), F),
    }


def reference(x, ffn1_norm, ffn1_w1, ffn1_w3, ffn1_w2, mix_norm, w_in, conv_a, conv_b,
              ln_b_gain, ln_b_bias, pool_map, pool_scale, q_norm, k_norm, w_branch, w_gate,
              w_out, ffn2_norm, ffn2_w1, ffn2_w3, ffn2_w2):
    b, s, d = x.shape
    for l in range(DEPTH):
        x = x + 0.5 * swiglu(rms_norm(x, ffn1_norm[l]), ffn1_w1[l], ffn1_w3[l], ffn1_w2[l])

        h = rms_norm(x, mix_norm[l])
        p = (h @ w_in[l]).reshape(b, s, N_IN_SLOTS, BRANCH_WIDTH)
        a_x, a_b, a_c, b_val, b_gate, c_in, q, k, v = [p[:, :, i] for i in range(N_IN_SLOTS)]

        y_a = a_b * causal_depthwise_conv(a_c * a_x, conv_a[l])

        y_b = jax.nn.silu(layer_norm(
            causal_depthwise_conv(b_val * jax.nn.sigmoid(b_gate), conv_b[l]),
            ln_b_gain[l], ln_b_bias[l]))

        pooled = multiscale_pool(c_in).reshape(b, s, len(POOL_WINDOWS), POOL_GROUP)
        y_c = jnp.einsum("bsgc,gce->bsge", pooled, pool_map[l]).reshape(b, s, BRANCH_WIDTH)
        y_c = y_c * pool_scale[l]

        qh = rms_norm(q.reshape(b, s, SB_HEADS, SB_HEAD_DIM), q_norm[l])
        kh = rms_norm(k.reshape(b, s, SB_HEADS, SB_HEAD_DIM), k_norm[l])
        vh = v.reshape(b, s, SB_HEADS, SB_HEAD_DIM)
        y_d = stick_breaking_attention(qh, kh, vh)

        branches = jnp.stack([y_a, y_b, y_c, y_d], axis=2)
        up = jnp.einsum("bsnc,ncd->bsnd", branches, w_branch[l])
        gates = jax.nn.sigmoid(h @ w_gate[l]).reshape(b, s, N_BRANCH, d)
        x = x + jnp.sum(gates * up, axis=2) @ w_out[l]

        x = x + 0.5 * swiglu(rms_norm(x, ffn2_norm[l]), ffn2_w1[l], ffn2_w3[l], ffn2_w2[l])
    return x
```

```python
import functools

import jax
import jax.numpy as jnp
from jax import lax
from jax.experimental import pallas as pl
from jax.experimental.pallas import tpu as pltpu

F32 = jnp.float32
BF16 = jnp.bfloat16

EPS = 1e-6
N_BRANCH = 4
WIDTH = 512
N_SLOTS = 9
SC_TAPS = 3
CONF_TAPS = 31
POOL_WINDOWS = (2, 4, 8, 16)
POOL_GROUP = WIDTH // len(POOL_WINDOWS)
HEADS = 8
HEAD_DIM = WIDTH // HEADS
LANES = 128
HEADS_PER_TILE = LANES // HEAD_DIM

FFN_TM = 1024
FFN_TF = 512
FFN_RC = 256
MIX_TM = 256
HALO = 32
MIX_RC = 64
ATT_TQ = 256
ATT_TK = 256
GATE_TM = 1024
GATE_TN = 1024
PROJ_TM = 512
VMEM_LIMIT = 56 * 1024 * 1024


def _rms(x, g):
    return x * lax.rsqrt(jnp.mean(x * x, axis=-1, keepdims=True) + EPS) * g


def _ffn_kernel(x_ref, g_ref, w1_ref, w3_ref, w2_ref, o_ref, h_ref):
    j = pl.program_id(1)
    tm = x_ref.shape[0]

    @pl.when(j == 0)
    def _():
        def body(c, carry):
            r = pl.multiple_of(c * FFN_RC, FFN_RC)
            xs = x_ref[pl.ds(r, FFN_RC), :]
            h_ref[pl.ds(r, FFN_RC), :] = _rms(xs, g_ref[...]).astype(BF16)
            o_ref[pl.ds(r, FFN_RC), :] = xs
            return carry
        lax.fori_loop(0, tm // FFN_RC, body, 0)

    for c in range(tm // FFN_RC):
        rows = pl.ds(c * FFN_RC, FFN_RC)
        h = h_ref[rows, :]
        a = jnp.dot(h, w1_ref[...], preferred_element_type=F32)
        b = jnp.dot(h, w3_ref[...], preferred_element_type=F32)
        act = (a * jax.nn.sigmoid(a)) * b * 0.5
        o_ref[rows, :] += jnp.dot(act.astype(BF16), w2_ref[...], preferred_element_type=F32)


def _ffn(x, g, w1, w3, w2):
    s, d = x.shape
    f = w1.shape[1]
    return pl.pallas_call(
        _ffn_kernel,
        grid=(s // FFN_TM, f // FFN_TF),
        in_specs=[
            pl.BlockSpec((FFN_TM, d), lambda i, j: (i, 0)),
            pl.BlockSpec((1, d), lambda i, j: (0, 0)),
            pl.BlockSpec((d, FFN_TF), lambda i, j: (0, j)),
            pl.BlockSpec((d, FFN_TF), lambda i, j: (0, j)),
            pl.BlockSpec((FFN_TF, d), lambda i, j: (j, 0)),
        ],
        out_specs=pl.BlockSpec((FFN_TM, d), lambda i, j: (i, 0)),
        out_shape=jax.ShapeDtypeStruct((s, d), F32),
        scratch_shapes=[pltpu.VMEM((FFN_TM, d), BF16)],
        compiler_params=pltpu.CompilerParams(
            dimension_semantics=("arbitrary", "arbitrary"), vmem_limit_bytes=VMEM_LIMIT),
        name="ffn",
    )(x, g, w1, w3, w2)


def _mix_kernel(x_ref, g_ref, win_ref, ca_ref, cb_ref, lng_ref, lnb_ref, pm_ref, ps_ref,
                qg_ref, kg_ref, bd_ref,
                h_ref, ya_ref, yb_ref, yc_ref, q_ref, k_ref, v_ref,
                p_ref, ext_ref, cv_ref):
    i = pl.program_id(0)
    tm = x_ref.shape[0]
    w = WIDTH

    @pl.when(i == 0)
    def _():
        ext_ref[:, 0:HALO, :] = jnp.zeros((3, HALO, w), F32)

    @pl.when(i > 0)
    def _():
        ext_ref[:, 0:HALO, :] = ext_ref[:, tm:tm + HALO, :]

    h = _rms(x_ref[...], g_ref[...]).astype(BF16)
    h_ref[...] = h
    for s in range(N_SLOTS):
        cols = slice(s * w, (s + 1) * w)
        p_ref[:, cols] = jnp.dot(h, win_ref[:, cols], preferred_element_type=F32)

    def slot(s):
        return p_ref[:, s * w:(s + 1) * w]

    ext_ref[0, HALO:HALO + tm, :] = slot(2) * slot(0)
    ext_ref[1, HALO:HALO + tm, :] = slot(3) * jax.nn.sigmoid(slot(4))
    ext_ref[2, HALO:HALO + tm, :] = slot(5)

    def conv(b, taps_ref, taps, r0, lanes):
        acc = None
        for k in range(taps):
            start = HALO + r0 - (taps - 1) + k
            term = taps_ref[k:k + 1, lanes] * ext_ref[b, start:start + MIX_RC, lanes]
            acc = term if acc is None else acc + term
        return acc

    row = lax.broadcasted_iota(jnp.int32, (MIX_RC, LANES), 0)
    for r0 in range(0, tm, MIX_RC):
        rows = slice(r0, r0 + MIX_RC)
        for g, win in enumerate(POOL_WINDOWS):
            lanes = slice(g * LANES, (g + 1) * LANES)
            ya_ref[rows, lanes] = (p_ref[rows, w + g * LANES:w + (g + 1) * LANES]
                                   * conv(0, ca_ref, SC_TAPS, r0, lanes)).astype(BF16)
            cv_ref[rows, lanes] = conv(1, cb_ref, CONF_TAPS, r0, lanes)
            u = ext_ref[2, HALO + r0:HALO + r0 + MIX_RC, lanes]
            tot = u
            for back in range(1, win):
                tot = tot + ext_ref[2, HALO + r0 - back:HALO + r0 - back + MIX_RC, lanes]
            pos = i * tm + r0 + row
            cnt = jnp.minimum(pos + 1, win).astype(F32)
            pooled = tot / cnt - u
            yc = jnp.dot(pooled.astype(BF16), pm_ref[g], preferred_element_type=F32)
            yc_ref[rows, lanes] = (yc * ps_ref[:, lanes]).astype(BF16)

    cv = cv_ref[...]
    mu = jnp.mean(cv, axis=-1, keepdims=True)
    xc = cv - mu
    ln = xc * lax.rsqrt(jnp.mean(xc * xc, axis=-1, keepdims=True) + EPS)
    ln = ln * lng_ref[...] + lnb_ref[...]
    yb_ref[...] = (ln * jax.nn.sigmoid(ln)).astype(BF16)

    def head_rms(x, gain):
        sq = x * x
        hi = sq.astype(BF16)
        lo = (sq - hi.astype(F32)).astype(BF16)
        ss = (jnp.dot(hi, bd_ref[...], preferred_element_type=F32)
              + jnp.dot(lo, bd_ref[...], preferred_element_type=F32))
        return x * lax.rsqrt(ss * (1.0 / HEAD_DIM) + EPS) * gain

    q_ref[...] = (head_rms(slot(6), qg_ref[...]) * (HEAD_DIM ** -0.5)).astype(BF16)
    k_ref[...] = head_rms(slot(7), kg_ref[...]).astype(BF16)
    v_ref[...] = slot(8).astype(BF16)


def _mix(x, g, w_in, conv_a, conv_b, ln_g, ln_b, pool_map, pool_scale, q_gain, k_gain, blockdiag):
    s, d = x.shape
    w = WIDTH
    tm = MIX_TM
    const2 = lambda i: (0, 0)
    row_blk = lambda i: (i, 0)
    out_w = jax.ShapeDtypeStruct((s, w), BF16)
    return pl.pallas_call(
        _mix_kernel,
        grid=(s // tm,),
        in_specs=[
            pl.BlockSpec((tm, d), row_blk),
            pl.BlockSpec((1, d), const2),
            pl.BlockSpec((d, N_SLOTS * w), const2, pipeline_mode=pl.Buffered(1)),
            pl.BlockSpec((SC_TAPS, w), const2),
            pl.BlockSpec((CONF_TAPS, w), const2),
            pl.BlockSpec((1, w), const2),
            pl.BlockSpec((1, w), const2),
            pl.BlockSpec((len(POOL_WINDOWS), POOL_GROUP, POOL_GROUP), lambda i: (0, 0, 0)),
            pl.BlockSpec((1, w), const2),
            pl.BlockSpec((1, w), const2),
            pl.BlockSpec((1, w), const2),
            pl.BlockSpec((w, w), const2),
        ],
        out_specs=[pl.BlockSpec((tm, d), row_blk)] + [pl.BlockSpec((tm, w), row_blk)] * 6,
        out_shape=[jax.ShapeDtypeStruct((s, d), BF16)] + [out_w] * 6,
        scratch_shapes=[
            pltpu.VMEM((tm, N_SLOTS * w), F32),
            pltpu.VMEM((3, HALO + tm, w), F32),
            pltpu.VMEM((tm, w), F32),
        ],
        compiler_params=pltpu.CompilerParams(
            dimension_semantics=("arbitrary",), vmem_limit_bytes=VMEM_LIMIT),
        name="mix",
    )(x, g, w_in, conv_a, conv_b, ln_g, ln_b, pool_map, pool_scale, q_gain, k_gain, blockdiag)


def _attn_kernel(q_ref, k_ref, v_ref, u_ref, o_ref, qm_ref, acc_ref, carry_ref):
    qi = pl.program_id(0)
    tq, tk = ATT_TQ, ATT_TK
    n_tiles = WIDTH // LANES

    lane = lax.broadcasted_iota(jnp.int32, (tq, LANES), 1)
    for t in range(n_tiles):
        qt = q_ref[:, t * LANES:(t + 1) * LANES]
        for hh in range(HEADS_PER_TILE):
            mine = (lane >= hh * HEAD_DIM) & (lane < (hh + 1) * HEAD_DIM)
            qm_ref[t * HEADS_PER_TILE + hh] = jnp.where(mine, qt, jnp.zeros_like(qt))
    acc_ref[...] = jnp.zeros_like(acc_ref)
    carry_ref[...] = jnp.zeros_like(carry_ref)

    def step(kj, diagonal):
        k0 = pl.multiple_of(kj * tk, tk)
        if diagonal:
            r = lax.broadcasted_iota(jnp.int32, (tq, tk), 0) + qi * tq
            c = lax.broadcasted_iota(jnp.int32, (tq, tk), 1) + k0
            mask = c < r
        for t in range(n_tiles):
            lanes = slice(t * LANES, (t + 1) * LANES)
            kt = k_ref[pl.ds(k0, tk), lanes]
            vt = v_ref[pl.ds(k0, tk), lanes]
            pv = []
            for hh in range(HEADS_PER_TILE):
                hd = t * HEADS_PER_TILE + hh
                z = lax.dot_general(qm_ref[hd], kt, (((1,), (1,)), ((), ())),
                                    preferred_element_type=F32)
                lg = -(jnp.maximum(z, 0.0) + jnp.log1p(jnp.exp(-jnp.abs(z))))
                if diagonal:
                    lg = jnp.where(mask, lg, 0.0)
                cs = jnp.dot(lg.astype(BF16), u_ref[...], preferred_element_type=F32)
                carry = carry_ref[hd]
                parts = [(z[:, b:b + LANES] + lg[:, b:b + LANES]) + (cs[:, b:b + LANES] + carry)
                         for b in range(0, tk, LANES)]
                a = jnp.exp(jnp.concatenate(parts, axis=1))
                if diagonal:
                    a = jnp.where(mask, a, 0.0)
                carry_ref[hd] = carry + cs[:, tk:tk + LANES]
                pv.append(jnp.dot(a.astype(BF16), vt, preferred_element_type=F32))
            out = pv[-1]
            for hh in range(HEADS_PER_TILE - 2, -1, -1):
                out = jnp.where(lane < (hh + 1) * HEAD_DIM, pv[hh], out)
            acc_ref[t] += out

    diag = ((qi + 1) * tq - 1) // tk
    step(diag, True)

    def body(n, c):
        step(diag - 1 - n, False)
        return c
    lax.fori_loop(0, diag, body, 0)

    for t in range(n_tiles):
        o_ref[:, t * LANES:(t + 1) * LANES] = acc_ref[t].astype(BF16)


def _attn(q, k, v, tri):
    s, w = q.shape
    tq, tk = ATT_TQ, ATT_TK
    return pl.pallas_call(
        _attn_kernel,
        grid=(s // tq,),
        in_specs=[
            pl.BlockSpec((tq, w), lambda i: (i, 0)),
            pl.BlockSpec((s, w), lambda i: (0, 0)),
            pl.BlockSpec((s, w), lambda i: (0, 0)),
            pl.BlockSpec((tk, tk + LANES), lambda i: (0, 0)),
        ],
        out_specs=pl.BlockSpec((tq, w), lambda i: (i, 0)),
        out_shape=jax.ShapeDtypeStruct((s, w), BF16),
        scratch_shapes=[
            pltpu.VMEM((HEADS, tq, LANES), BF16),
            pltpu.VMEM((w // LANES, tq, LANES), F32),
            pltpu.VMEM((HEADS, tq, LANES), F32),
        ],
        compiler_params=pltpu.CompilerParams(
            dimension_semantics=("arbitrary",), vmem_limit_bytes=VMEM_LIMIT),
        name="attn",
    )(q, k, v, tri)


def _gate_kernel(h_ref, ya_ref, yb_ref, yc_ref, yd_ref, wg_ref, wb_ref, o_ref, acc_ref):
    b = pl.program_id(2)
    gate = jax.nn.sigmoid(jnp.dot(h_ref[...], wg_ref[...], preferred_element_type=F32))

    for n, y_ref in enumerate((ya_ref, yb_ref, yc_ref, yd_ref)):
        @pl.when(b == n)
        def _(n=n, y_ref=y_ref):
            term = gate * jnp.dot(y_ref[...], wb_ref[0], preferred_element_type=F32)
            if n == 0:
                acc_ref[...] = term
            elif n < N_BRANCH - 1:
                acc_ref[...] += term
            else:
                o_ref[...] = (acc_ref[...] + term).astype(BF16)


def _gate(h, ys, w_gate, w_branch):
    s, d = h.shape
    w = WIDTH
    tm, tn = GATE_TM, GATE_TN
    nn = d // tn
    y_spec = pl.BlockSpec((tm, w), lambda i, n, b: (i, 0))
    return pl.pallas_call(
        _gate_kernel,
        grid=(s // tm, nn, N_BRANCH),
        in_specs=[
            pl.BlockSpec((tm, d), lambda i, n, b: (i, 0)),
            y_spec, y_spec, y_spec, y_spec,
            pl.BlockSpec((d, tn), lambda i, n, b: (0, b * nn + n)),
            pl.BlockSpec((1, w, tn), lambda i, n, b: (b, 0, n)),
        ],
        out_specs=pl.BlockSpec((tm, tn), lambda i, n, b: (i, n)),
        out_shape=jax.ShapeDtypeStruct((s, d), BF16),
        scratch_shapes=[pltpu.VMEM((tm, tn), F32)],
        compiler_params=pltpu.CompilerParams(
            dimension_semantics=("arbitrary", "arbitrary", "arbitrary"),
            vmem_limit_bytes=VMEM_LIMIT),
        name="gate",
    )(h, *ys, w_gate, w_branch)


def _proj_kernel(x_ref, m_ref, w_ref, o_ref):
    o_ref[...] = x_ref[...] + jnp.dot(m_ref[...], w_ref[...], preferred_element_type=F32)


def _proj(x, merged, w_out):
    s, d = x.shape
    tm = PROJ_TM
    return pl.pallas_call(
        _proj_kernel,
        grid=(s // tm,),
        in_specs=[
            pl.BlockSpec((tm, d), lambda i: (i, 0)),
            pl.BlockSpec((tm, d), lambda i: (i, 0)),
            pl.BlockSpec((d, d), lambda i: (0, 0), pipeline_mode=pl.Buffered(1)),
        ],
        out_specs=pl.BlockSpec((tm, d), lambda i: (i, 0)),
        out_shape=jax.ShapeDtypeStruct((s, d), F32),
        compiler_params=pltpu.CompilerParams(
            dimension_semantics=("arbitrary",), vmem_limit_bytes=VMEM_LIMIT),
        name="proj",
    )(x, merged, w_out)


def _constants():
    j = lax.broadcasted_iota(jnp.int32, (ATT_TK, ATT_TK + LANES), 0)
    c = lax.broadcasted_iota(jnp.int32, (ATT_TK, ATT_TK + LANES), 1)
    tri = ((j > c) | (c >= ATT_TK)).astype(BF16)
    hr = lax.broadcasted_iota(jnp.int32, (WIDTH, WIDTH), 0) // HEAD_DIM
    hc = lax.broadcasted_iota(jnp.int32, (WIDTH, WIDTH), 1) // HEAD_DIM
    blockdiag = (hr == hc).astype(BF16)
    return tri, blockdiag


def kernel(x, ffn1_norm, ffn1_w1, ffn1_w3, ffn1_w2, mix_norm, w_in, conv_a, conv_b, ln_b_gain,
           ln_b_bias, pool_map, pool_scale, q_norm, k_norm, w_branch, w_gate, w_out, ffn2_norm,
           ffn2_w1, ffn2_w3, ffn2_w2):
    b, s, d = x.shape
    depth = w_in.shape[0]
    tri, blockdiag = _constants()
    bf = lambda a: a.astype(BF16)
    row = lambda a: a.reshape(1, -1)
    outs = []
    for bi in range(b):
        xb = x[bi]
        for l in range(depth):
            xb = _ffn(xb, row(ffn1_norm[l]), bf(ffn1_w1[l]), bf(ffn1_w3[l]), bf(ffn1_w2[l]))
            h, ya, yb, yc, q, k, v = _mix(
                xb, row(mix_norm[l]), bf(w_in[l]), conv_a[l], conv_b[l], row(ln_b_gain[l]),
                row(ln_b_bias[l]), bf(pool_map[l]), row(pool_scale[l]),
                row(jnp.tile(q_norm[l], HEADS)), row(jnp.tile(k_norm[l], HEADS)), blockdiag)
            yd = _attn(q, k, v, tri)
            merged = _gate(h, (ya, yb, yc, yd), bf(w_gate[l]), bf(w_branch[l]))
            xb = _proj(xb, merged, bf(w_out[l]))
            xb = _ffn(xb, row(ffn2_norm[l]), bf(ffn2_w1[l]), bf(ffn2_w3[l]), bf(ffn2_w2[l]))
        outs.append(xb)
    return jnp.stack(outs, axis=0)
```

```python
import functools

import jax
import jax.numpy as jnp
from jax import lax
from jax.experimental import pallas as pl
from jax.experimental.pallas import tpu as pltpu

F32 = jnp.float32
BF16 = jnp.bfloat16

EPS = 1e-6
LOG2E = 1.4426950408889634
N_BRANCH = 4
WIDTH = 512
N_SLOTS = 9
SC_TAPS = 3
CONF_TAPS = 31
POOL_WINDOWS = (2, 4, 8, 16)
POOL_GROUP = WIDTH // len(POOL_WINDOWS)
HEADS = 8
HEAD_DIM = WIDTH // HEADS
LANES = 128
HEADS_PER_TILE = LANES // HEAD_DIM

FFN_TM = 1024
FFN_TF = 512
FFN_RC = 256
MIX_TM = 256
HALO = 32
MIX_RC = 64
ATT_TQ = 256
ATT_TK = 256
ATT_SKEW = 2
GATE_TM = 1024
GATE_TN = 1024
PROJ_TM = 512
VMEM_LIMIT = 56 * 1024 * 1024


def _rms(x, g):
    return x * lax.rsqrt(jnp.mean(x * x, axis=-1, keepdims=True) + EPS) * g


def _ffn_kernel(x_ref, g_ref, w1_ref, w3_ref, w2_ref, o_ref, h_ref):
    j = pl.program_id(1)
    tm = x_ref.shape[0]

    @pl.when(j == 0)
    def _():
        def body(c, carry):
            r = pl.multiple_of(c * FFN_RC, FFN_RC)
            xs = x_ref[pl.ds(r, FFN_RC), :]
            h_ref[pl.ds(r, FFN_RC), :] = _rms(xs, g_ref[...]).astype(BF16)
            o_ref[pl.ds(r, FFN_RC), :] = xs
            return carry
        lax.fori_loop(0, tm // FFN_RC, body, 0)

    for c in range(tm // FFN_RC):
        rows = pl.ds(c * FFN_RC, FFN_RC)
        h = h_ref[rows, :]
        a = jnp.dot(h, w1_ref[...], preferred_element_type=F32)
        b = jnp.dot(h, w3_ref[...], preferred_element_type=F32)
        act = (a * jax.nn.sigmoid(a)) * b * 0.5
        o_ref[rows, :] += jnp.dot(act.astype(BF16), w2_ref[...], preferred_element_type=F32)


def _ffn(x, g, w1, w3, w2):
    s, d = x.shape
    f = w1.shape[1]
    return pl.pallas_call(
        _ffn_kernel,
        grid=(s // FFN_TM, f // FFN_TF),
        in_specs=[
            pl.BlockSpec((FFN_TM, d), lambda i, j: (i, 0)),
            pl.BlockSpec((1, d), lambda i, j: (0, 0)),
            pl.BlockSpec((d, FFN_TF), lambda i, j: (0, j)),
            pl.BlockSpec((d, FFN_TF), lambda i, j: (0, j)),
            pl.BlockSpec((FFN_TF, d), lambda i, j: (j, 0)),
        ],
        out_specs=pl.BlockSpec((FFN_TM, d), lambda i, j: (i, 0)),
        out_shape=jax.ShapeDtypeStruct((s, d), F32),
        scratch_shapes=[pltpu.VMEM((FFN_TM, d), BF16)],
        compiler_params=pltpu.CompilerParams(
            dimension_semantics=("arbitrary", "arbitrary"), vmem_limit_bytes=VMEM_LIMIT),
        name="ffn",
    )(x, g, w1, w3, w2)


def _mix_kernel(x_ref, g_ref, win_ref, ca_ref, cb_ref, lng_ref, lnb_ref, pm_ref, ps_ref,
                qg_ref, kg_ref, bd_ref,
                h_ref, ya_ref, yb_ref, yc_ref, q_ref, k_ref, v_ref,
                p_ref, ext_ref, cv_ref):
    i = pl.program_id(0)
    tm = x_ref.shape[0]
    w = WIDTH

    @pl.when(i == 0)
    def _():
        ext_ref[:, 0:HALO, :] = jnp.zeros((3, HALO, w), F32)

    @pl.when(i > 0)
    def _():
        ext_ref[:, 0:HALO, :] = ext_ref[:, tm:tm + HALO, :]

    h = _rms(x_ref[...], g_ref[...]).astype(BF16)
    h_ref[...] = h
    for s in range(N_SLOTS):
        cols = slice(s * w, (s + 1) * w)
        p_ref[:, cols] = jnp.dot(h, win_ref[:, cols], preferred_element_type=F32)

    def slot(s):
        return p_ref[:, s * w:(s + 1) * w]

    ext_ref[0, HALO:HALO + tm, :] = slot(2) * slot(0)
    ext_ref[1, HALO:HALO + tm, :] = slot(3) * jax.nn.sigmoid(slot(4))
    ext_ref[2, HALO:HALO + tm, :] = slot(5)

    def conv(b, taps_ref, taps, r0, lanes):
        acc = None
        for k in range(taps):
            start = HALO + r0 - (taps - 1) + k
            term = taps_ref[k:k + 1, lanes] * ext_ref[b, start:start + MIX_RC, lanes]
            acc = term if acc is None else acc + term
        return acc

    row = lax.broadcasted_iota(jnp.int32, (MIX_RC, LANES), 0)
    for r0 in range(0, tm, MIX_RC):
        rows = slice(r0, r0 + MIX_RC)
        for g, win in enumerate(POOL_WINDOWS):
            lanes = slice(g * LANES, (g + 1) * LANES)
            ya_ref[rows, lanes] = (p_ref[rows, w + g * LANES:w + (g + 1) * LANES]
                                   * conv(0, ca_ref, SC_TAPS, r0, lanes)).astype(BF16)
            cv_ref[rows, lanes] = conv(1, cb_ref, CONF_TAPS, r0, lanes)
            u = ext_ref[2, HALO + r0:HALO + r0 + MIX_RC, lanes]
            tot = u
            for back in range(1, win):
                tot = tot + ext_ref[2, HALO + r0 - back:HALO + r0 - back + MIX_RC, lanes]
            pos = i * tm + r0 + row
            cnt = jnp.minimum(pos + 1, win).astype(F32)
            pooled = tot / cnt - u
            yc = jnp.dot(pooled.astype(BF16), pm_ref[g], preferred_element_type=F32)
            yc_ref[rows, lanes] = (yc * ps_ref[:, lanes]).astype(BF16)

    cv = cv_ref[...]
    mu = jnp.mean(cv, axis=-1, keepdims=True)
    xc = cv - mu
    ln = xc * lax.rsqrt(jnp.mean(xc * xc, axis=-1, keepdims=True) + EPS)
    ln = ln * lng_ref[...] + lnb_ref[...]
    yb_ref[...] = (ln * jax.nn.sigmoid(ln)).astype(BF16)

    def head_rms(x, gain):
        sq = x * x
        hi = sq.astype(BF16)
        lo = (sq - hi.astype(F32)).astype(BF16)
        ss = (jnp.dot(hi, bd_ref[...], preferred_element_type=F32)
              + jnp.dot(lo, bd_ref[...], preferred_element_type=F32))
        return x * lax.rsqrt(ss * (1.0 / HEAD_DIM) + EPS) * gain

    q_ref[...] = (head_rms(slot(6), qg_ref[...]) * (LOG2E * HEAD_DIM ** -0.5)).astype(BF16)
    k_ref[...] = head_rms(slot(7), kg_ref[...]).astype(BF16)
    v_ref[...] = slot(8).astype(BF16)


def _mix(x, g, w_in, conv_a, conv_b, ln_g, ln_b, pool_map, pool_scale, q_gain, k_gain, blockdiag):
    s, d = x.shape
    w = WIDTH
    tm = MIX_TM
    const2 = lambda i: (0, 0)
    row_blk = lambda i: (i, 0)
    out_w = jax.ShapeDtypeStruct((s, w), BF16)
    return pl.pallas_call(
        _mix_kernel,
        grid=(s // tm,),
        in_specs=[
            pl.BlockSpec((tm, d), row_blk),
            pl.BlockSpec((1, d), const2),
            pl.BlockSpec((d, N_SLOTS * w), const2, pipeline_mode=pl.Buffered(1)),
            pl.BlockSpec((SC_TAPS, w), const2),
            pl.BlockSpec((CONF_TAPS, w), const2),
            pl.BlockSpec((1, w), const2),
            pl.BlockSpec((1, w), const2),
            pl.BlockSpec((len(POOL_WINDOWS), POOL_GROUP, POOL_GROUP), lambda i: (0, 0, 0)),
            pl.BlockSpec((1, w), const2),
            pl.BlockSpec((1, w), const2),
            pl.BlockSpec((1, w), const2),
            pl.BlockSpec((w, w), const2),
        ],
        out_specs=[pl.BlockSpec((tm, d), row_blk)] + [pl.BlockSpec((tm, w), row_blk)] * 6,
        out_shape=[jax.ShapeDtypeStruct((s, d), BF16)] + [out_w] * 6,
        scratch_shapes=[
            pltpu.VMEM((tm, N_SLOTS * w), F32),
            pltpu.VMEM((3, HALO + tm, w), F32),
            pltpu.VMEM((tm, w), F32),
        ],
        compiler_params=pltpu.CompilerParams(
            dimension_semantics=("arbitrary",), vmem_limit_bytes=VMEM_LIMIT),
        name="mix",
    )(x, g, w_in, conv_a, conv_b, ln_g, ln_b, pool_map, pool_scale, q_gain, k_gain, blockdiag)


def _attn_kernel(q_ref, k_ref, v_ref, u_ref, o_ref, qm_ref, acc_ref, carry_ref,
                 gain_ref, cost_ref, prob_ref):
    qi = pl.program_id(0)
    tq, tk = ATT_TQ, ATT_TK
    n_tiles = WIDTH // LANES
    hpt = HEADS_PER_TILE

    lane = lax.broadcasted_iota(jnp.int32, (tq, LANES), 1)
    for t in range(n_tiles):
        qt = q_ref[:, t * LANES:(t + 1) * LANES]
        for hh in range(hpt):
            mine = (lane >= hh * HEAD_DIM) & (lane < (hh + 1) * HEAD_DIM)
            qm_ref[t, hh * tq:(hh + 1) * tq, :] = jnp.where(mine, qt, jnp.zeros_like(qt))
    acc_ref[...] = jnp.zeros_like(acc_ref)
    carry_ref[...] = jnp.zeros_like(carry_ref)

    def step(kj, diagonal):
        k0 = pl.multiple_of(kj * tk, tk)
        if diagonal:
            r = lax.broadcasted_iota(jnp.int32, (tq, tk), 0) + qi * tq
            c = lax.broadcasted_iota(jnp.int32, (tq, tk), 1) + k0
            mask = jnp.concatenate([c < r] * hpt, axis=0)

        def scores(t):
            return lax.dot_general(qm_ref[t], k_ref[pl.ds(k0, tk), t * LANES:(t + 1) * LANES],
                                   (((1,), (1,)), ((), ())), preferred_element_type=F32)

        def costs(t, z):
            lg = jnp.log2(1.0 + jnp.exp2(-jnp.abs(z)))
            cost = jnp.maximum(z, 0.0) + lg
            gain_ref[t] = z - cost
            if diagonal:
                cost = jnp.where(mask, cost, 0.0)
            cost_ref[t] = cost.astype(BF16)
            tot = jnp.sum(cost, axis=1, keepdims=True)
            return tot, jnp.dot(cost_ref[t], u_ref[...], preferred_element_type=F32)

        def weights(t, tot, within):
            carry = carry_ref[t]
            a = jnp.exp2(gain_ref[t] - (within + carry))
            if diagonal:
                a = jnp.where(mask, a, 0.0)
            carry_ref[t] = carry + tot
            prob_ref[t] = a.astype(BF16)
            pv = jnp.dot(prob_ref[t], v_ref[pl.ds(k0, tk), t * LANES:(t + 1) * LANES],
                         preferred_element_type=F32)
            out = pv[(hpt - 1) * tq:hpt * tq]
            for hh in range(hpt - 2, -1, -1):
                out = jnp.where(lane < (hh + 1) * HEAD_DIM, pv[hh * tq:(hh + 1) * tq], out)
            acc_ref[t] += out

        zs, mids = {}, {}
        for s in range(n_tiles + 2 * ATT_SKEW):
            if s < n_tiles:
                zs[s] = scores(s)
            if 0 <= s - ATT_SKEW < n_tiles:
                mids[s - ATT_SKEW] = costs(s - ATT_SKEW, zs.pop(s - ATT_SKEW))
            if 0 <= s - 2 * ATT_SKEW < n_tiles:
                weights(s - 2 * ATT_SKEW, *mids.pop(s - 2 * ATT_SKEW))

    diag = ((qi + 1) * tq - 1) // tk
    step(diag, True)

    def body(n, c):
        step(diag - 1 - n, False)
        return c
    lax.fori_loop(0, diag, body, 0)

    for t in range(n_tiles):
        o_ref[:, t * LANES:(t + 1) * LANES] = acc_ref[t].astype(BF16)


def _attn(q, k, v, tri):
    s, w = q.shape
    tq, tk = ATT_TQ, ATT_TK
    nt, hpt = w // LANES, HEADS_PER_TILE
    return pl.pallas_call(
        _attn_kernel,
        grid=(s // tq,),
        in_specs=[
            pl.BlockSpec((tq, w), lambda i: (i, 0)),
            pl.BlockSpec((s, w), lambda i: (0, 0)),
            pl.BlockSpec((s, w), lambda i: (0, 0)),
            pl.BlockSpec((tk, tk), lambda i: (0, 0)),
        ],
        out_specs=pl.BlockSpec((tq, w), lambda i: (i, 0)),
        out_shape=jax.ShapeDtypeStruct((s, w), BF16),
        scratch_shapes=[
            pltpu.VMEM((nt, hpt * tq, LANES), BF16),
            pltpu.VMEM((nt, tq, LANES), F32),
            pltpu.VMEM((nt, hpt * tq, 1), F32),
            pltpu.VMEM((nt, hpt * tq, tk), F32),
            pltpu.VMEM((nt, hpt * tq, tk), BF16),
            pltpu.VMEM((nt, hpt * tq, tk), BF16),
        ],
        compiler_params=pltpu.CompilerParams(
            dimension_semantics=("arbitrary",), vmem_limit_bytes=VMEM_LIMIT),
        name="attn",
    )(q, k, v, tri)


def _gate_kernel(h_ref, ya_ref, yb_ref, yc_ref, yd_ref, wg_ref, wb_ref, o_ref, acc_ref):
    b = pl.program_id(2)
    gate = jax.nn.sigmoid(jnp.dot(h_ref[...], wg_ref[...], preferred_element_type=F32))

    for n, y_ref in enumerate((ya_ref, yb_ref, yc_ref, yd_ref)):
        @pl.when(b == n)
        def _(n=n, y_ref=y_ref):
            term = gate * jnp.dot(y_ref[...], wb_ref[0], preferred_element_type=F32)
            if n == 0:
                acc_ref[...] = term
            elif n < N_BRANCH - 1:
                acc_ref[...] += term
            else:
                o_ref[...] = (acc_ref[...] + term).astype(BF16)


def _gate(h, ys, w_gate, w_branch):
    s, d = h.shape
    w = WIDTH
    tm, tn = GATE_TM, GATE_TN
    nn = d // tn
    y_spec = pl.BlockSpec((tm, w), lambda i, n, b: (i, 0))
    return pl.pallas_call(
        _gate_kernel,
        grid=(s // tm, nn, N_BRANCH),
        in_specs=[
            pl.BlockSpec((tm, d), lambda i, n, b: (i, 0)),
            y_spec, y_spec, y_spec, y_spec,
            pl.BlockSpec((d, tn), lambda i, n, b: (0, b * nn + n)),
            pl.BlockSpec((1, w, tn), lambda i, n, b: (b, 0, n)),
        ],
        out_specs=pl.BlockSpec((tm, tn), lambda i, n, b: (i, n)),
        out_shape=jax.ShapeDtypeStruct((s, d), BF16),
        scratch_shapes=[pltpu.VMEM((tm, tn), F32)],
        compiler_params=pltpu.CompilerParams(
            dimension_semantics=("arbitrary", "arbitrary", "arbitrary"),
            vmem_limit_bytes=VMEM_LIMIT),
        name="gate",
    )(h, *ys, w_gate, w_branch)


def _proj_kernel(x_ref, m_ref, w_ref, o_ref):
    o_ref[...] = x_ref[...] + jnp.dot(m_ref[...], w_ref[...], preferred_element_type=F32)


def _proj(x, merged, w_out):
    s, d = x.shape
    tm = PROJ_TM
    return pl.pallas_call(
        _proj_kernel,
        grid=(s // tm,),
        in_specs=[
            pl.BlockSpec((tm, d), lambda i: (i, 0)),
            pl.BlockSpec((tm, d), lambda i: (i, 0)),
            pl.BlockSpec((d, d), lambda i: (0, 0), pipeline_mode=pl.Buffered(1)),
        ],
        out_specs=pl.BlockSpec((tm, d), lambda i: (i, 0)),
        out_shape=jax.ShapeDtypeStruct((s, d), F32),
        compiler_params=pltpu.CompilerParams(
            dimension_semantics=("arbitrary",), vmem_limit_bytes=VMEM_LIMIT),
        name="proj",
    )(x, merged, w_out)


def _constants():
    j = lax.broadcasted_iota(jnp.int32, (ATT_TK, ATT_TK), 0)
    c = lax.broadcasted_iota(jnp.int32, (ATT_TK, ATT_TK), 1)
    tri = (j > c).astype(BF16)
    hr = lax.broadcasted_iota(jnp.int32, (WIDTH, WIDTH), 0) // HEAD_DIM
    hc = lax.broadcasted_iota(jnp.int32, (WIDTH, WIDTH), 1) // HEAD_DIM
    blockdiag = (hr == hc).astype(BF16)
    return tri, blockdiag


def kernel(x, ffn1_norm, ffn1_w1, ffn1_w3, ffn1_w2, mix_norm, w_in, conv_a, conv_b, ln_b_gain,
           ln_b_bias, pool_map, pool_scale, q_norm, k_norm, w_branch, w_gate, w_out, ffn2_norm,
           ffn2_w1, ffn2_w3, ffn2_w2):
    b, s, d = x.shape
    depth = w_in.shape[0]
    tri, blockdiag = _constants()
    bf = lambda a: a.astype(BF16)
    row = lambda a: a.reshape(1, -1)
    outs = []
    for bi in range(b):
        xb = x[bi]
        for l in range(depth):
            xb = _ffn(xb, row(ffn1_norm[l]), bf(ffn1_w1[l]), bf(ffn1_w3[l]), bf(ffn1_w2[l]))
            h, ya, yb, yc, q, k, v = _mix(
                xb, row(mix_norm[l]), bf(w_in[l]), conv_a[l], conv_b[l], row(ln_b_gain[l]),
                row(ln_b_bias[l]), bf(pool_map[l]), row(pool_scale[l]),
                row(jnp.tile(q_norm[l], HEADS)), row(jnp.tile(k_norm[l], HEADS)), blockdiag)
            yd = _attn(q, k, v, tri)
            merged = _gate(h, (ya, yb, yc, yd), bf(w_gate[l]), bf(w_branch[l]))
            xb = _proj(xb, merged, bf(w_out[l]))
            xb = _ffn(xb, row(ffn2_norm[l]), bf(ffn2_w1[l]), bf(ffn2_w3[l]), bf(ffn2_w2[l]))
        outs.append(xb)
    return jnp.stack(outs, axis=0)
```

```python
import functools

import jax
import jax.numpy as jnp
from jax import lax
from jax.experimental import pallas as pl
from jax.experimental.pallas import tpu as pltpu

F32 = jnp.float32
BF16 = jnp.bfloat16

EPS = 1e-6
LOG2E = 1.4426950408889634
N_BRANCH = 4
WIDTH = 512
N_SLOTS = 9
SC_TAPS = 3
CONF_TAPS = 31
POOL_WINDOWS = (2, 4, 8, 16)
POOL_GROUP = WIDTH // len(POOL_WINDOWS)
HEADS = 8
HEAD_DIM = WIDTH // HEADS
LANES = 128
SUBLANES = 8
HEADS_PER_TILE = LANES // HEAD_DIM

FFN_TM = 1024
FFN_TF = 512
FFN_RC = 256
MIX_TM = 256
HALO = 32
MIX_RC = 64
ATT_TQ = 256
ATT_TK = 256
ATT_SKEW = 2
GATE_TM = 1024
GATE_TN = 1024
PROJ_TM = 512
VMEM_LIMIT = 56 * 1024 * 1024


def _rms(x, g):
    return x * lax.rsqrt(jnp.mean(x * x, axis=-1, keepdims=True) + EPS) * g


def _ffn_kernel(x_ref, g_ref, w1_ref, w3_ref, w2_ref, o_ref, h_ref):
    j = pl.program_id(1)
    tm = x_ref.shape[0]

    @pl.when(j == 0)
    def _():
        def body(c, carry):
            r = pl.multiple_of(c * FFN_RC, FFN_RC)
            xs = x_ref[pl.ds(r, FFN_RC), :]
            h_ref[pl.ds(r, FFN_RC), :] = _rms(xs, g_ref[...]).astype(BF16)
            o_ref[pl.ds(r, FFN_RC), :] = xs
            return carry
        lax.fori_loop(0, tm // FFN_RC, body, 0)

    for c in range(tm // FFN_RC):
        rows = pl.ds(c * FFN_RC, FFN_RC)
        h = h_ref[rows, :]
        a = jnp.dot(h, w1_ref[...], preferred_element_type=F32)
        b = jnp.dot(h, w3_ref[...], preferred_element_type=F32)
        act = (a * jax.nn.sigmoid(a)) * b * 0.5
        o_ref[rows, :] += jnp.dot(act.astype(BF16), w2_ref[...], preferred_element_type=F32)


def _ffn(x, g, w1, w3, w2, l):
    s, d = x.shape
    f = w1.shape[2]
    return pl.pallas_call(
        _ffn_kernel,
        grid=(s // FFN_TM, f // FFN_TF),
        in_specs=[
            pl.BlockSpec((FFN_TM, d), lambda i, j: (i, 0)),
            pl.BlockSpec((None, 1, d), lambda i, j: (l, 0, 0)),
            pl.BlockSpec((None, d, FFN_TF), lambda i, j: (l, 0, j)),
            pl.BlockSpec((None, d, FFN_TF), lambda i, j: (l, 0, j)),
            pl.BlockSpec((None, FFN_TF, d), lambda i, j: (l, j, 0)),
        ],
        out_specs=pl.BlockSpec((FFN_TM, d), lambda i, j: (i, 0)),
        out_shape=jax.ShapeDtypeStruct((s, d), F32),
        scratch_shapes=[pltpu.VMEM((FFN_TM, d), BF16)],
        compiler_params=pltpu.CompilerParams(
            dimension_semantics=("arbitrary", "arbitrary"), vmem_limit_bytes=VMEM_LIMIT),
        name="ffn",
    )(x, g, w1, w3, w2)


def _mix_kernel(x_ref, g_ref, win_ref, ca_ref, cb_ref, lng_ref, lnb_ref, pm_ref, ps_ref,
                qg_ref, kg_ref, bd_ref,
                h_ref, ya_ref, yb_ref, yc_ref, q_ref, k_ref, v_ref,
                p_ref, ext_ref, sh_ref, cv_ref):
    i = pl.program_id(0)
    tm = x_ref.shape[0]
    w = WIDTH

    @pl.when(i == 0)
    def _():
        ext_ref[:, 0:HALO, :] = jnp.zeros((3, HALO, w), F32)

    @pl.when(i > 0)
    def _():
        ext_ref[:, 0:HALO, :] = ext_ref[:, tm:tm + HALO, :]

    h_ref[...] = _rms(x_ref[...], g_ref[...]).astype(BF16)
    for s in range(N_SLOTS):
        cols = slice(s * w, (s + 1) * w)
        p_ref[:, cols] = jnp.dot(h_ref[...], win_ref[:, cols], preferred_element_type=F32)

    def slot(s, rows=slice(None), lanes=slice(0, w)):
        return p_ref[rows, s * w + lanes.start:s * w + lanes.stop]

    ext_ref[0, HALO:HALO + tm, :] = slot(2) * slot(0)
    ext_ref[1, HALO:HALO + tm, :] = slot(3) * jax.nn.sigmoid(slot(4))
    ext_ref[2, HALO:HALO + tm, :] = slot(5)

    n_sh = HALO + tm - SUBLANES
    for r in range(1, SUBLANES):
        sh_ref[r - 1] = ext_ref[1, r:r + n_sh, :]

    def conv_a(r0, lanes):
        acc = None
        for k in range(SC_TAPS):
            start = HALO + r0 - (SC_TAPS - 1) + k
            term = ca_ref[k:k + 1, lanes] * ext_ref[0, start:start + MIX_RC, lanes]
            acc = term if acc is None else acc + term
        return acc

    def conv_b(r0, lanes):
        acc = None
        for k in range(CONF_TAPS):
            start = HALO + r0 - (CONF_TAPS - 1) + k
            r = start % SUBLANES
            base = start - r
            if r == 0:
                u = ext_ref[1, base:base + MIX_RC, lanes]
            else:
                u = sh_ref[r - 1, base:base + MIX_RC, lanes]
            term = cb_ref[k:k + 1, lanes] * u
            acc = term if acc is None else acc + term
        return acc

    row = lax.broadcasted_iota(jnp.int32, (MIX_RC, LANES), 0)
    for r0 in range(0, tm, MIX_RC):
        rows = slice(r0, r0 + MIX_RC)
        for g, win in enumerate(POOL_WINDOWS):
            lanes = slice(g * LANES, (g + 1) * LANES)
            ya_ref[rows, lanes] = (slot(1, rows, lanes) * conv_a(r0, lanes)).astype(BF16)
            cv_ref[rows, lanes] = conv_b(r0, lanes)
            u = ext_ref[2, HALO + r0:HALO + r0 + MIX_RC, lanes]
            tot = u
            for back in range(1, win):
                tot = tot + ext_ref[2, HALO + r0 - back:HALO + r0 - back + MIX_RC, lanes]
            pos = i * tm + r0 + row
            cnt = jnp.minimum(pos + 1, win).astype(F32)
            pooled = tot / cnt - u
            yc = jnp.dot(pooled.astype(BF16), pm_ref[g], preferred_element_type=F32)
            yc_ref[rows, lanes] = (yc * ps_ref[:, lanes]).astype(BF16)

    cv = cv_ref[...]
    mu = jnp.mean(cv, axis=-1, keepdims=True)
    xc = cv - mu
    ln = xc * lax.rsqrt(jnp.mean(xc * xc, axis=-1, keepdims=True) + EPS)
    ln = ln * lng_ref[...] + lnb_ref[...]
    yb_ref[...] = (ln * jax.nn.sigmoid(ln)).astype(BF16)

    def head_rms(x, gain):
        sq = x * x
        hi = sq.astype(BF16)
        lo = (sq - hi.astype(F32)).astype(BF16)
        ss = (jnp.dot(hi, bd_ref[...], preferred_element_type=F32)
              + jnp.dot(lo, bd_ref[...], preferred_element_type=F32))
        return x * lax.rsqrt(ss * (1.0 / HEAD_DIM) + EPS) * gain

    q_ref[...] = (head_rms(slot(6), qg_ref[...]) * (LOG2E * HEAD_DIM ** -0.5)).astype(BF16)
    k_ref[...] = head_rms(slot(7), kg_ref[...]).astype(BF16)
    v_ref[...] = slot(8).astype(BF16)


def _mix(x, g, w_in, conv_a, conv_b, ln_g, ln_b, pool_map, pool_scale, q_gain, k_gain, blockdiag, l):
    s, d = x.shape
    w = WIDTH
    tm = MIX_TM
    layer3 = lambda i: (l, 0, 0)
    row_blk = lambda i: (i, 0)
    out_w = jax.ShapeDtypeStruct((s, w), BF16)
    return pl.pallas_call(
        _mix_kernel,
        grid=(s // tm,),
        in_specs=[
            pl.BlockSpec((tm, d), row_blk),
            pl.BlockSpec((None, 1, d), layer3),
            pl.BlockSpec((None, d, N_SLOTS * w), layer3, pipeline_mode=pl.Buffered(1)),
            pl.BlockSpec((None, SC_TAPS, w), layer3),
            pl.BlockSpec((None, CONF_TAPS, w), layer3),
            pl.BlockSpec((None, 1, w), layer3),
            pl.BlockSpec((None, 1, w), layer3),
            pl.BlockSpec((None, len(POOL_WINDOWS), POOL_GROUP, POOL_GROUP), lambda i: (l, 0, 0, 0)),
            pl.BlockSpec((None, 1, w), layer3),
            pl.BlockSpec((None, 1, w), layer3),
            pl.BlockSpec((None, 1, w), layer3),
            pl.BlockSpec((w, w), lambda i: (0, 0)),
        ],
        out_specs=[pl.BlockSpec((tm, d), row_blk)] + [pl.BlockSpec((tm, w), row_blk)] * 6,
        out_shape=[jax.ShapeDtypeStruct((s, d), BF16)] + [out_w] * 6,
        scratch_shapes=[
            pltpu.VMEM((tm, N_SLOTS * w), F32),
            pltpu.VMEM((3, HALO + tm, w), F32),
            pltpu.VMEM((SUBLANES - 1, HALO + tm - SUBLANES, w), F32),
            pltpu.VMEM((tm, w), F32),
        ],
        compiler_params=pltpu.CompilerParams(
            dimension_semantics=("arbitrary",), vmem_limit_bytes=VMEM_LIMIT),
        name="mix",
    )(x, g, w_in, conv_a, conv_b, ln_g, ln_b, pool_map, pool_scale, q_gain, k_gain, blockdiag)


def _attn_kernel(q_ref, k_ref, v_ref, u_ref, o_ref, qm_ref, acc_ref, carry_ref,
                 gain_ref, cost_ref, prob_ref):
    qi = pl.program_id(0)
    tq, tk = ATT_TQ, ATT_TK
    n_tiles = WIDTH // LANES
    hpt = HEADS_PER_TILE

    lane = lax.broadcasted_iota(jnp.int32, (tq, LANES), 1)
    for t in range(n_tiles):
        qt = q_ref[:, t * LANES:(t + 1) * LANES]
        for hh in range(hpt):
            mine = (lane >= hh * HEAD_DIM) & (lane < (hh + 1) * HEAD_DIM)
            qm_ref[t, hh * tq:(hh + 1) * tq, :] = jnp.where(mine, qt, jnp.zeros_like(qt))
    acc_ref[...] = jnp.zeros_like(acc_ref)
    carry_ref[...] = jnp.zeros_like(carry_ref)

    def step(kj, diagonal):
        k0 = pl.multiple_of(kj * tk, tk)
        if diagonal:
            r = lax.broadcasted_iota(jnp.int32, (tq, tk), 0) + qi * tq
            c = lax.broadcasted_iota(jnp.int32, (tq, tk), 1) + k0
            mask = jnp.concatenate([c < r] * hpt, axis=0)

        def scores(t):
            return lax.dot_general(qm_ref[t], k_ref[pl.ds(k0, tk), t * LANES:(t + 1) * LANES],
                                   (((1,), (1,)), ((), ())), preferred_element_type=F32)

        def costs(t, z):
            lg = jnp.log2(1.0 + jnp.exp2(-jnp.abs(z)))
            cost = jnp.maximum(z, 0.0) + lg
            gain_ref[t] = z - cost
            if diagonal:
                cost = jnp.where(mask, cost, 0.0)
            cost_ref[t] = cost.astype(BF16)
            tot = jnp.sum(cost, axis=1, keepdims=True)
            return tot, jnp.dot(cost_ref[t], u_ref[...], preferred_element_type=F32)

        def weights(t, tot, within):
            carry = carry_ref[t]
            a = jnp.exp2(gain_ref[t] - (within + carry))
            if diagonal:
                a = jnp.where(mask, a, 0.0)
            carry_ref[t] = carry + tot
            prob_ref[t] = a.astype(BF16)
            pv = jnp.dot(prob_ref[t], v_ref[pl.ds(k0, tk), t * LANES:(t + 1) * LANES],
                         preferred_element_type=F32)
            out = pv[(hpt - 1) * tq:hpt * tq]
            for hh in range(hpt - 2, -1, -1):
                out = jnp.where(lane < (hh + 1) * HEAD_DIM, pv[hh * tq:(hh + 1) * tq], out)
            acc_ref[t] += out

        zs, mids = {}, {}
        for s in range(n_tiles + 2 * ATT_SKEW):
            if s < n_tiles:
                zs[s] = scores(s)
            if 0 <= s - ATT_SKEW < n_tiles:
                mids[s - ATT_SKEW] = costs(s - ATT_SKEW, zs.pop(s - ATT_SKEW))
            if 0 <= s - 2 * ATT_SKEW < n_tiles:
                weights(s - 2 * ATT_SKEW, *mids.pop(s - 2 * ATT_SKEW))

    diag = ((qi + 1) * tq - 1) // tk
    step(diag, True)

    def body(n, c):
        step(diag - 1 - n, False)
        return c
    lax.fori_loop(0, diag, body, 0)

    for t in range(n_tiles):
        o_ref[:, t * LANES:(t + 1) * LANES] = acc_ref[t].astype(BF16)


def _attn(q, k, v, tri):
    s, w = q.shape
    tq, tk = ATT_TQ, ATT_TK
    nt, hpt = w // LANES, HEADS_PER_TILE
    return pl.pallas_call(
        _attn_kernel,
        grid=(s // tq,),
        in_specs=[
            pl.BlockSpec((tq, w), lambda i: (i, 0)),
            pl.BlockSpec((s, w), lambda i: (0, 0)),
            pl.BlockSpec((s, w), lambda i: (0, 0)),
            pl.BlockSpec((tk, tk), lambda i: (0, 0)),
        ],
        out_specs=pl.BlockSpec((tq, w), lambda i: (i, 0)),
        out_shape=jax.ShapeDtypeStruct((s, w), BF16),
        scratch_shapes=[
            pltpu.VMEM((nt, hpt * tq, LANES), BF16),
            pltpu.VMEM((nt, tq, LANES), F32),
            pltpu.VMEM((nt, hpt * tq, 1), F32),
            pltpu.VMEM((nt, hpt * tq, tk), F32),
            pltpu.VMEM((nt, hpt * tq, tk), BF16),
            pltpu.VMEM((nt, hpt * tq, tk), BF16),
        ],
        compiler_params=pltpu.CompilerParams(
            dimension_semantics=("arbitrary",), vmem_limit_bytes=VMEM_LIMIT),
        name="attn",
    )(q, k, v, tri)


def _gate_kernel(h_ref, ya_ref, yb_ref, yc_ref, yd_ref, wg_ref, wb_ref, o_ref, acc_ref):
    b = pl.program_id(2)
    gate = jax.nn.sigmoid(jnp.dot(h_ref[...], wg_ref[...].astype(BF16),
                                  preferred_element_type=F32))

    for n, y_ref in enumerate((ya_ref, yb_ref, yc_ref, yd_ref)):
        @pl.when(b == n)
        def _(n=n, y_ref=y_ref):
            term = gate * jnp.dot(y_ref[...], wb_ref[...].astype(BF16),
                                  preferred_element_type=F32)
            if n == 0:
                acc_ref[...] = term
            elif n < N_BRANCH - 1:
                acc_ref[...] += term
            else:
                o_ref[...] = (acc_ref[...] + term).astype(BF16)


def _gate(h, ys, w_gate, w_branch, l):
    s, d = h.shape
    w = WIDTH
    tm, tn = GATE_TM, GATE_TN
    nn = d // tn
    y_spec = pl.BlockSpec((tm, w), lambda i, n, b: (i, 0))
    return pl.pallas_call(
        _gate_kernel,
        grid=(s // tm, nn, N_BRANCH),
        in_specs=[
            pl.BlockSpec((tm, d), lambda i, n, b: (i, 0)),
            y_spec, y_spec, y_spec, y_spec,
            pl.BlockSpec((None, d, tn), lambda i, n, b: (l, 0, b * nn + n)),
            pl.BlockSpec((None, None, w, tn), lambda i, n, b: (l, b, 0, n)),
        ],
        out_specs=pl.BlockSpec((tm, tn), lambda i, n, b: (i, n)),
        out_shape=jax.ShapeDtypeStruct((s, d), BF16),
        scratch_shapes=[pltpu.VMEM((tm, tn), F32)],
        compiler_params=pltpu.CompilerParams(
            dimension_semantics=("arbitrary", "arbitrary", "arbitrary"),
            vmem_limit_bytes=VMEM_LIMIT),
        name="gate",
    )(h, *ys, w_gate, w_branch)


def _proj_kernel(x_ref, m_ref, w_ref, o_ref):
    o_ref[...] = x_ref[...] + jnp.dot(m_ref[...], w_ref[...].astype(BF16),
                                      preferred_element_type=F32)


def _proj(x, merged, w_out, l):
    s, d = x.shape
    tm = PROJ_TM
    return pl.pallas_call(
        _proj_kernel,
        grid=(s // tm,),
        in_specs=[
            pl.BlockSpec((tm, d), lambda i: (i, 0)),
            pl.BlockSpec((tm, d), lambda i: (i, 0)),
            pl.BlockSpec((None, d, d), lambda i: (l, 0, 0), pipeline_mode=pl.Buffered(1)),
        ],
        out_specs=pl.BlockSpec((tm, d), lambda i: (i, 0)),
        out_shape=jax.ShapeDtypeStruct((s, d), F32),
        compiler_params=pltpu.CompilerParams(
            dimension_semantics=("arbitrary",), vmem_limit_bytes=VMEM_LIMIT),
        name="proj",
    )(x, merged, w_out)


def _constants():
    j = lax.broadcasted_iota(jnp.int32, (ATT_TK, ATT_TK), 0)
    c = lax.broadcasted_iota(jnp.int32, (ATT_TK, ATT_TK), 1)
    tri = (j > c).astype(BF16)
    hr = lax.broadcasted_iota(jnp.int32, (WIDTH, WIDTH), 0) // HEAD_DIM
    hc = lax.broadcasted_iota(jnp.int32, (WIDTH, WIDTH), 1) // HEAD_DIM
    blockdiag = (hr == hc).astype(BF16)
    return tri, blockdiag


def kernel(x, ffn1_norm, ffn1_w1, ffn1_w3, ffn1_w2, mix_norm, w_in, conv_a, conv_b, ln_b_gain,
           ln_b_bias, pool_map, pool_scale, q_norm, k_norm, w_branch, w_gate, w_out, ffn2_norm,
           ffn2_w1, ffn2_w3, ffn2_w2):
    b, s, d = x.shape
    depth = w_in.shape[0]
    tri, blockdiag = _constants()
    bf = lambda a: a.astype(BF16)
    row = lambda a: a.reshape(depth, 1, -1)
    f1w1, f1w3, f1w2 = bf(ffn1_w1), bf(ffn1_w3), bf(ffn1_w2)
    f2w1, f2w3, f2w2 = bf(ffn2_w1), bf(ffn2_w3), bf(ffn2_w2)
    win, pmap = bf(w_in), bf(pool_map)
    f1g, f2g, mg = row(ffn1_norm), row(ffn2_norm), row(mix_norm)
    lng, lnb, psc = row(ln_b_gain), row(ln_b_bias), row(pool_scale)
    qg, kg = row(jnp.tile(q_norm, (1, HEADS))), row(jnp.tile(k_norm, (1, HEADS)))
    outs = []
    for bi in range(b):
        xb = x[bi]
        for l in range(depth):
            xb = _ffn(xb, f1g, f1w1, f1w3, f1w2, l)
            h, ya, yb, yc, q, k, v = _mix(xb, mg, win, conv_a, conv_b, lng, lnb, pmap, psc,
                                          qg, kg, blockdiag, l)
            yd = _attn(q, k, v, tri)
            merged = _gate(h, (ya, yb, yc, yd), w_gate, w_branch, l)
            xb = _proj(xb, merged, w_out, l)
            xb = _ffn(xb, f2g, f2w1, f2w3, f2w2, l)
        outs.append(xb)
    return jnp.stack(outs, axis=0)
```

```python
import functools

import jax
import jax.numpy as jnp
from jax import lax
from jax.experimental import pallas as pl
from jax.experimental.pallas import tpu as pltpu

F32 = jnp.float32
BF16 = jnp.bfloat16

EPS = 1e-6
LOG2E = 1.4426950408889634
N_BRANCH = 4
WIDTH = 512
N_SLOTS = 9
SC_TAPS = 3
CONF_TAPS = 31
POOL_WINDOWS = (2, 4, 8, 16)
POOL_GROUP = WIDTH // len(POOL_WINDOWS)
HEADS = 8
HEAD_DIM = WIDTH // HEADS
LANES = 128
SUBLANES = 8
HEADS_PER_TILE = LANES // HEAD_DIM

FFN_TM = 1024
FFN_TF = 512
FFN_RC = 256
MIX_TM = 256
HALO = 32
MIX_RC = 64
ATT_TQ = 256
ATT_TK = 256
ATT_SKEW = 3
ATT_UNROLL = 2
GATE_TM = 1024
GATE_TN = 1024
PROJ_TM = 512
VMEM_LIMIT = 56 * 1024 * 1024


def _rms(x, g):
    return x * lax.rsqrt(jnp.mean(x * x, axis=-1, keepdims=True) + EPS) * g


def _ffn_kernel(x_hbm, g_ref, w1_ref, w3_ref, w2_ref, o_ref, h_ref, wb_ref, sem):
    i = pl.program_id(0)
    j = pl.program_id(1)
    tm = o_ref.shape[0]
    n_chunks = tm // FFN_RC

    def x_copy(c):
        rows = pl.ds(c * FFN_RC, FFN_RC)
        return pltpu.make_async_copy(x_hbm.at[pl.ds(i * tm + c * FFN_RC, FFN_RC), :],
                                     o_ref.at[rows, :], sem.at[c])

    @pl.when(j == 0)
    def _():
        for c in range(n_chunks):
            x_copy(c).start()
        for c in range(n_chunks):
            x_copy(c).wait()
            rows = pl.ds(c * FFN_RC, FFN_RC)
            h_ref[rows, :] = _rms(o_ref[rows, :], g_ref[...]).astype(BF16)

    wb_ref[0] = w1_ref[...].astype(BF16)
    wb_ref[1] = w3_ref[...].astype(BF16)
    w2b = w2_ref[...].astype(BF16)
    for c in range(n_chunks):
        rows = pl.ds(c * FFN_RC, FFN_RC)
        h = h_ref[rows, :]
        a = jnp.dot(h, wb_ref[0], preferred_element_type=F32)
        b = jnp.dot(h, wb_ref[1], preferred_element_type=F32)
        act = (a * jax.nn.sigmoid(a)) * b * 0.5
        o_ref[rows, :] += jnp.dot(act.astype(BF16), w2b, preferred_element_type=F32)


def _ffn(x, g, w1, w3, w2, l):
    s, d = x.shape
    f = w1.shape[2]
    return pl.pallas_call(
        _ffn_kernel,
        grid=(s // FFN_TM, f // FFN_TF),
        in_specs=[
            pl.BlockSpec(memory_space=pl.ANY),
            pl.BlockSpec((None, 1, d), lambda i, j: (l, 0, 0)),
            pl.BlockSpec((None, d, FFN_TF), lambda i, j: (l, 0, j)),
            pl.BlockSpec((None, d, FFN_TF), lambda i, j: (l, 0, j)),
            pl.BlockSpec((None, FFN_TF, d), lambda i, j: (l, j, 0)),
        ],
        out_specs=pl.BlockSpec((FFN_TM, d), lambda i, j: (i, 0)),
        out_shape=jax.ShapeDtypeStruct((s, d), F32),
        scratch_shapes=[
            pltpu.VMEM((FFN_TM, d), BF16),
            pltpu.VMEM((2, d, FFN_TF), BF16),
            pltpu.SemaphoreType.DMA((FFN_TM // FFN_RC,)),
        ],
        compiler_params=pltpu.CompilerParams(
            dimension_semantics=("arbitrary", "arbitrary"), vmem_limit_bytes=VMEM_LIMIT),
        name="ffn",
    )(x, g, w1, w3, w2)


def _mix_kernel(x_ref, g_ref, win_ref, ca_ref, cb_ref, lng_ref, lnb_ref, pm_ref, ps_ref,
                qg_ref, kg_ref, bd_ref,
                h_ref, ya_ref, yb_ref, yc_ref, q_ref, k_ref, v_ref,
                p_ref, ext_ref, sh_ref, cv_ref):
    i = pl.program_id(0)
    tm = x_ref.shape[0]
    w = WIDTH

    @pl.when(i == 0)
    def _():
        ext_ref[:, 0:HALO, :] = jnp.zeros((3, HALO, w), F32)

    @pl.when(i > 0)
    def _():
        ext_ref[:, 0:HALO, :] = ext_ref[:, tm:tm + HALO, :]

    h_ref[...] = _rms(x_ref[...], g_ref[...]).astype(BF16)
    for s in range(N_SLOTS):
        cols = slice(s * w, (s + 1) * w)
        p_ref[:, cols] = jnp.dot(h_ref[...], win_ref[:, cols], preferred_element_type=F32)

    def slot(s, rows=slice(None), lanes=slice(0, w)):
        return p_ref[rows, s * w + lanes.start:s * w + lanes.stop]

    ext_ref[0, HALO:HALO + tm, :] = slot(2) * slot(0)
    ext_ref[1, HALO:HALO + tm, :] = slot(3) * jax.nn.sigmoid(slot(4))
    ext_ref[2, HALO:HALO + tm, :] = slot(5)

    n_sh = HALO + tm - SUBLANES
    for r in range(1, SUBLANES):
        sh_ref[r - 1] = ext_ref[1, r:r + n_sh, :]

    def conv_a(r0, lanes):
        acc = None
        for k in range(SC_TAPS):
            start = HALO + r0 - (SC_TAPS - 1) + k
            term = ca_ref[k:k + 1, lanes] * ext_ref[0, start:start + MIX_RC, lanes]
            acc = term if acc is None else acc + term
        return acc

    def conv_b(r0, lanes):
        acc = None
        for k in range(CONF_TAPS):
            start = HALO + r0 - (CONF_TAPS - 1) + k
            r = start % SUBLANES
            base = start - r
            if r == 0:
                u = ext_ref[1, base:base + MIX_RC, lanes]
            else:
                u = sh_ref[r - 1, base:base + MIX_RC, lanes]
            term = cb_ref[k:k + 1, lanes] * u
            acc = term if acc is None else acc + term
        return acc

    row = lax.broadcasted_iota(jnp.int32, (MIX_RC, LANES), 0)
    for r0 in range(0, tm, MIX_RC):
        rows = slice(r0, r0 + MIX_RC)
        for g, win in enumerate(POOL_WINDOWS):
            lanes = slice(g * LANES, (g + 1) * LANES)
            ya_ref[rows, lanes] = (slot(1, rows, lanes) * conv_a(r0, lanes)).astype(BF16)
            cv_ref[rows, lanes] = conv_b(r0, lanes)
            u = ext_ref[2, HALO + r0:HALO + r0 + MIX_RC, lanes]
            tot = u
            for back in range(1, win):
                tot = tot + ext_ref[2, HALO + r0 - back:HALO + r0 - back + MIX_RC, lanes]
            pos = i * tm + r0 + row
            cnt = jnp.minimum(pos + 1, win).astype(F32)
            pooled = tot / cnt - u
            yc = jnp.dot(pooled.astype(BF16), pm_ref[g], preferred_element_type=F32)
            yc_ref[rows, lanes] = (yc * ps_ref[:, lanes]).astype(BF16)

    cv = cv_ref[...]
    mu = jnp.mean(cv, axis=-1, keepdims=True)
    xc = cv - mu
    ln = xc * lax.rsqrt(jnp.mean(xc * xc, axis=-1, keepdims=True) + EPS)
    ln = ln * lng_ref[...] + lnb_ref[...]
    yb_ref[...] = (ln * jax.nn.sigmoid(ln)).astype(BF16)

    def head_rms(x, gain):
        sq = x * x
        hi = sq.astype(BF16)
        lo = (sq - hi.astype(F32)).astype(BF16)
        ss = (jnp.dot(hi, bd_ref[...], preferred_element_type=F32)
              + jnp.dot(lo, bd_ref[...], preferred_element_type=F32))
        return x * lax.rsqrt(ss * (1.0 / HEAD_DIM) + EPS) * gain

    q_ref[...] = (head_rms(slot(6), qg_ref[...]) * (LOG2E * HEAD_DIM ** -0.5)).astype(BF16)
    k_ref[...] = head_rms(slot(7), kg_ref[...]).astype(BF16)
    v_ref[...] = slot(8).astype(BF16)


def _mix(x, g, w_in, conv_a, conv_b, ln_g, ln_b, pool_map, pool_scale, q_gain, k_gain, blockdiag, l):
    s, d = x.shape
    w = WIDTH
    tm = MIX_TM
    layer3 = lambda i: (l, 0, 0)
    row_blk = lambda i: (i, 0)
    out_w = jax.ShapeDtypeStruct((s, w), BF16)
    return pl.pallas_call(
        _mix_kernel,
        grid=(s // tm,),
        in_specs=[
            pl.BlockSpec((tm, d), row_blk),
            pl.BlockSpec((None, 1, d), layer3),
            pl.BlockSpec((None, d, N_SLOTS * w), layer3, pipeline_mode=pl.Buffered(1)),
            pl.BlockSpec((None, SC_TAPS, w), layer3),
            pl.BlockSpec((None, CONF_TAPS, w), layer3),
            pl.BlockSpec((None, 1, w), layer3),
            pl.BlockSpec((None, 1, w), layer3),
            pl.BlockSpec((None, len(POOL_WINDOWS), POOL_GROUP, POOL_GROUP), lambda i: (l, 0, 0, 0)),
            pl.BlockSpec((None, 1, w), layer3),
            pl.BlockSpec((None, 1, w), layer3),
            pl.BlockSpec((None, 1, w), layer3),
            pl.BlockSpec((w, w), lambda i: (0, 0)),
        ],
        out_specs=[pl.BlockSpec((tm, d), row_blk)] + [pl.BlockSpec((tm, w), row_blk)] * 6,
        out_shape=[jax.ShapeDtypeStruct((s, d), BF16)] + [out_w] * 6,
        scratch_shapes=[
            pltpu.VMEM((tm, N_SLOTS * w), F32),
            pltpu.VMEM((3, HALO + tm, w), F32),
            pltpu.VMEM((SUBLANES - 1, HALO + tm - SUBLANES, w), F32),
            pltpu.VMEM((tm, w), F32),
        ],
        compiler_params=pltpu.CompilerParams(
            dimension_semantics=("arbitrary",), vmem_limit_bytes=VMEM_LIMIT),
        name="mix",
    )(x, g, w_in, conv_a, conv_b, ln_g, ln_b, pool_map, pool_scale, q_gain, k_gain, blockdiag)


def _attn_kernel(q_ref, k_ref, v_ref, u_ref, o_ref, qm_ref, acc_ref, carry_ref,
                 gain_ref, cost_ref, prob_ref):
    qi = pl.program_id(0)
    tq, tk = ATT_TQ, ATT_TK
    n_tiles = WIDTH // LANES
    hpt = HEADS_PER_TILE

    lane = lax.broadcasted_iota(jnp.int32, (tq, LANES), 1)
    for t in range(n_tiles):
        qt = q_ref[:, t * LANES:(t + 1) * LANES]
        for hh in range(hpt):
            mine = (lane >= hh * HEAD_DIM) & (lane < (hh + 1) * HEAD_DIM)
            qm_ref[t, hh * tq:(hh + 1) * tq, :] = jnp.where(mine, qt, jnp.zeros_like(qt))
    acc_ref[...] = jnp.zeros_like(acc_ref)
    carry_ref[...] = jnp.zeros_like(carry_ref)

    def step(kjs, diagonal):
        k0s = [pl.multiple_of(kj * tk, tk) for kj in kjs]
        if diagonal:
            r = lax.broadcasted_iota(jnp.int32, (tq, tk), 0) + qi * tq
            c = lax.broadcasted_iota(jnp.int32, (tq, tk), 1) + k0s[0]
            mask = jnp.concatenate([c < r] * hpt, axis=0)
        units = [(kk, t) for kk in range(len(kjs)) for t in range(n_tiles)]

        def scores(u):
            kk, t = units[u]
            return lax.dot_general(qm_ref[t], k_ref[pl.ds(k0s[kk], tk), t * LANES:(t + 1) * LANES],
                                   (((1,), (1,)), ((), ())), preferred_element_type=F32)

        def costs(u, z):
            lg = jnp.log2(1.0 + jnp.exp2(-jnp.abs(z)))
            cost = jnp.maximum(z, 0.0) + lg
            gain_ref[u] = z - cost
            if diagonal:
                cost = jnp.where(mask, cost, 0.0)
            cost_ref[u] = cost.astype(BF16)
            tot = jnp.sum(cost, axis=1, keepdims=True)
            return tot, jnp.dot(cost_ref[u], u_ref[...], preferred_element_type=F32)

        def weights(u, tot, within):
            kk, t = units[u]
            carry = carry_ref[t]
            a = jnp.exp2(gain_ref[u] - (within + carry))
            if diagonal:
                a = jnp.where(mask, a, 0.0)
            carry_ref[t] = carry + tot
            prob_ref[u] = a.astype(BF16)
            pv = jnp.dot(prob_ref[u], v_ref[pl.ds(k0s[kk], tk), t * LANES:(t + 1) * LANES],
                         preferred_element_type=F32)
            out = pv[(hpt - 1) * tq:hpt * tq]
            for hh in range(hpt - 2, -1, -1):
                out = jnp.where(lane < (hh + 1) * HEAD_DIM, pv[hh * tq:(hh + 1) * tq], out)
            acc_ref[t] += out

        zs, mids = {}, {}
        for s in range(len(units) + 2 * ATT_SKEW):
            if s < len(units):
                zs[s] = scores(s)
            if 0 <= s - ATT_SKEW < len(units):
                mids[s - ATT_SKEW] = costs(s - ATT_SKEW, zs.pop(s - ATT_SKEW))
            if 0 <= s - 2 * ATT_SKEW < len(units):
                weights(s - 2 * ATT_SKEW, *mids.pop(s - 2 * ATT_SKEW))

    diag = ((qi + 1) * tq - 1) // tk
    step([diag], True)
    n_single = diag % ATT_UNROLL

    def single(n, c):
        step([diag - 1 - n], False)
        return c
    lax.fori_loop(0, n_single, single, 0)

    def group(n, c):
        top = diag - 1 - n_single - n * ATT_UNROLL
        step([top - kk for kk in range(ATT_UNROLL)], False)
        return c
    lax.fori_loop(0, diag // ATT_UNROLL, group, 0)

    for t in range(n_tiles):
        o_ref[:, t * LANES:(t + 1) * LANES] = acc_ref[t].astype(BF16)


def _attn(q, k, v, tri):
    s, w = q.shape
    tq, tk = ATT_TQ, ATT_TK
    nt, hpt = w // LANES, HEADS_PER_TILE
    return pl.pallas_call(
        _attn_kernel,
        grid=(s // tq,),
        in_specs=[
            pl.BlockSpec((tq, w), lambda i: (i, 0)),
            pl.BlockSpec((s, w), lambda i: (0, 0)),
            pl.BlockSpec((s, w), lambda i: (0, 0)),
            pl.BlockSpec((tk, tk), lambda i: (0, 0)),
        ],
        out_specs=pl.BlockSpec((tq, w), lambda i: (i, 0)),
        out_shape=jax.ShapeDtypeStruct((s, w), BF16),
        scratch_shapes=[
            pltpu.VMEM((nt, hpt * tq, LANES), BF16),
            pltpu.VMEM((nt, tq, LANES), F32),
            pltpu.VMEM((nt, hpt * tq, 1), F32),
            pltpu.VMEM((ATT_UNROLL * nt, hpt * tq, tk), F32),
            pltpu.VMEM((ATT_UNROLL * nt, hpt * tq, tk), BF16),
            pltpu.VMEM((ATT_UNROLL * nt, hpt * tq, tk), BF16),
        ],
        compiler_params=pltpu.CompilerParams(
            dimension_semantics=("arbitrary",), vmem_limit_bytes=VMEM_LIMIT),
        name="attn",
    )(q, k, v, tri)


def _gate_kernel(h_ref, ya_ref, yb_ref, yc_ref, yd_ref, wg_ref, wb_ref, o_ref, acc_ref):
    b = pl.program_id(2)
    gate = jax.nn.sigmoid(jnp.dot(h_ref[...], wg_ref[...].astype(BF16),
                                  preferred_element_type=F32))

    for n, y_ref in enumerate((ya_ref, yb_ref, yc_ref, yd_ref)):
        @pl.when(b == n)
        def _(n=n, y_ref=y_ref):
            term = gate * jnp.dot(y_ref[...], wb_ref[...].astype(BF16),
                                  preferred_element_type=F32)
            if n == 0:
                acc_ref[...] = term
            elif n < N_BRANCH - 1:
                acc_ref[...] += term
            else:
                o_ref[...] = (acc_ref[...] + term).astype(BF16)


def _gate(h, ys, w_gate, w_branch, l):
    s, d = h.shape
    w = WIDTH
    tm, tn = GATE_TM, GATE_TN
    nn = d // tn
    y_spec = pl.BlockSpec((tm, w), lambda i, n, b: (i, 0))
    return pl.pallas_call(
        _gate_kernel,
        grid=(s // tm, nn, N_BRANCH),
        in_specs=[
            pl.BlockSpec((tm, d), lambda i, n, b: (i, 0)),
            y_spec, y_spec, y_spec, y_spec,
            pl.BlockSpec((None, d, tn), lambda i, n, b: (l, 0, b * nn + n)),
            pl.BlockSpec((None, None, w, tn), lambda i, n, b: (l, b, 0, n)),
        ],
        out_specs=pl.BlockSpec((tm, tn), lambda i, n, b: (i, n)),
        out_shape=jax.ShapeDtypeStruct((s, d), BF16),
        scratch_shapes=[pltpu.VMEM((tm, tn), F32)],
        compiler_params=pltpu.CompilerParams(
            dimension_semantics=("arbitrary", "arbitrary", "arbitrary"),
            vmem_limit_bytes=VMEM_LIMIT),
        name="gate",
    )(h, *ys, w_gate, w_branch)


def _proj_kernel(x_ref, m_ref, w_ref, o_ref):
    o_ref[...] = x_ref[...] + jnp.dot(m_ref[...], w_ref[...].astype(BF16),
                                      preferred_element_type=F32)


def _proj(x, merged, w_out, l):
    s, d = x.shape
    tm = PROJ_TM
    return pl.pallas_call(
        _proj_kernel,
        grid=(s // tm,),
        in_specs=[
            pl.BlockSpec((tm, d), lambda i: (i, 0)),
            pl.BlockSpec((tm, d), lambda i: (i, 0)),
            pl.BlockSpec((None, d, d), lambda i: (l, 0, 0), pipeline_mode=pl.Buffered(1)),
        ],
        out_specs=pl.BlockSpec((tm, d), lambda i: (i, 0)),
        out_shape=jax.ShapeDtypeStruct((s, d), F32),
        compiler_params=pltpu.CompilerParams(
            dimension_semantics=("arbitrary",), vmem_limit_bytes=VMEM_LIMIT),
        name="proj",
    )(x, merged, w_out)


def _constants():
    j = lax.broadcasted_iota(jnp.int32, (ATT_TK, ATT_TK), 0)
    c = lax.broadcasted_iota(jnp.int32, (ATT_TK, ATT_TK), 1)
    tri = (j > c).astype(BF16)
    hr = lax.broadcasted_iota(jnp.int32, (WIDTH, WIDTH), 0) // HEAD_DIM
    hc = lax.broadcasted_iota(jnp.int32, (WIDTH, WIDTH), 1) // HEAD_DIM
    blockdiag = (hr == hc).astype(BF16)
    return tri, blockdiag


def kernel(x, ffn1_norm, ffn1_w1, ffn1_w3, ffn1_w2, mix_norm, w_in, conv_a, conv_b, ln_b_gain,
           ln_b_bias, pool_map, pool_scale, q_norm, k_norm, w_branch, w_gate, w_out, ffn2_norm,
           ffn2_w1, ffn2_w3, ffn2_w2):
    b, s, d = x.shape
    depth = w_in.shape[0]
    tri, blockdiag = _constants()
    bf = lambda a: a.astype(BF16)
    row = lambda a: a.reshape(depth, 1, -1)
    win, pmap = bf(w_in), bf(pool_map)
    f1g, f2g, mg = row(ffn1_norm), row(ffn2_norm), row(mix_norm)
    lng, lnb, psc = row(ln_b_gain), row(ln_b_bias), row(pool_scale)
    qg, kg = row(jnp.tile(q_norm, (1, HEADS))), row(jnp.tile(k_norm, (1, HEADS)))
    outs = []
    for bi in range(b):
        xb = x[bi]
        for l in range(depth):
            xb = _ffn(xb, f1g, ffn1_w1, ffn1_w3, ffn1_w2, l)
            h, ya, yb, yc, q, k, v = _mix(xb, mg, win, conv_a, conv_b, lng, lnb, pmap, psc,
                                          qg, kg, blockdiag, l)
            yd = _attn(q, k, v, tri)
            merged = _gate(h, (ya, yb, yc, yd), w_gate, w_branch, l)
            xb = _proj(xb, merged, w_out, l)
            xb = _ffn(xb, f2g, ffn2_w1, ffn2_w3, ffn2_w2, l)
        outs.append(xb)
    return jnp.stack(outs, axis=0)
```

```python
import functools

import jax
import jax.numpy as jnp
from jax import lax
from jax.experimental import pallas as pl
from jax.experimental.pallas import tpu as pltpu

F32 = jnp.float32
BF16 = jnp.bfloat16

EPS = 1e-6
LOG2E = 1.4426950408889634
N_BRANCH = 4
WIDTH = 512
N_SLOTS = 9
SC_TAPS = 3
CONF_TAPS = 31
POOL_WINDOWS = (2, 4, 8, 16)
POOL_GROUP = WIDTH // len(POOL_WINDOWS)
HEADS = 8
HEAD_DIM = WIDTH // HEADS
LANES = 128
SUBLANES = 8
HEADS_PER_TILE = LANES // HEAD_DIM

FFN_TM = 1024
FFN_TF = 512
FFN_RC = 256
MIX_TM = 256
HALO = 32
MIX_RC = 64
ATT_TQ = 256
ATT_TK = 256
ATT_SKEW = 3
ATT_UNROLL = 2
GATE_TM = 1024
GATE_TN = 1024
PROJ_TM = 512
VMEM_LIMIT = 56 * 1024 * 1024


def _rms(x, g):
    return x * lax.rsqrt(jnp.mean(x * x, axis=-1, keepdims=True) + EPS) * g


def _ffn_kernel(x_ref, g_ref, w1_ref, w3_ref, w2_ref, o_ref, h_ref):
    j = pl.program_id(1)
    tm = x_ref.shape[0]

    @pl.when(j == 0)
    def _():
        def body(c, carry):
            r = pl.multiple_of(c * FFN_RC, FFN_RC)
            xs = x_ref[pl.ds(r, FFN_RC), :]
            h_ref[pl.ds(r, FFN_RC), :] = _rms(xs, g_ref[...]).astype(BF16)
            o_ref[pl.ds(r, FFN_RC), :] = xs
            return carry
        lax.fori_loop(0, tm // FFN_RC, body, 0)

    for c in range(tm // FFN_RC):
        rows = pl.ds(c * FFN_RC, FFN_RC)
        h = h_ref[rows, :]
        a = jnp.dot(h, w1_ref[...], preferred_element_type=F32)
        b = jnp.dot(h, w3_ref[...], preferred_element_type=F32)
        act = (a * jax.nn.sigmoid(a)) * b * 0.5
        o_ref[rows, :] += jnp.dot(act.astype(BF16), w2_ref[...], preferred_element_type=F32)


def _ffn(x, g, l, w1, w3, w2):
    s, d = x.shape
    f = w1.shape[1]
    return pl.pallas_call(
        _ffn_kernel,
        grid=(s // FFN_TM, f // FFN_TF),
        in_specs=[
            pl.BlockSpec((FFN_TM, d), lambda i, j: (i, 0)),
            pl.BlockSpec((None, 1, d), lambda i, j: (l, 0, 0)),
            pl.BlockSpec((d, FFN_TF), lambda i, j: (0, j)),
            pl.BlockSpec((d, FFN_TF), lambda i, j: (0, j)),
            pl.BlockSpec((FFN_TF, d), lambda i, j: (j, 0)),
        ],
        out_specs=pl.BlockSpec((FFN_TM, d), lambda i, j: (i, 0)),
        out_shape=jax.ShapeDtypeStruct((s, d), F32),
        scratch_shapes=[pltpu.VMEM((FFN_TM, d), BF16)],
        compiler_params=pltpu.CompilerParams(
            dimension_semantics=("arbitrary", "arbitrary"), vmem_limit_bytes=VMEM_LIMIT),
        name="ffn",
    )(x, g, w1, w3, w2)


def _cast_specs(ws, l, n_steps):
    in_specs, out_specs, out_shapes = [], [], []
    for w in ws:
        _, r, c = w.shape
        rows = r // n_steps
        assert rows * n_steps == r and rows % 16 == 0, (r, n_steps)
        in_specs.append(pl.BlockSpec((None, rows, c), lambda i: (l, i, 0)))
        out_specs.append(pl.BlockSpec((rows, c), lambda i: (i, 0)))
        out_shapes.append(jax.ShapeDtypeStruct((r, c), BF16))
    return in_specs, out_specs, out_shapes


def _cast_slabs(in_refs, out_refs):
    for src, dst in zip(in_refs, out_refs):
        dst[...] = src[...].astype(BF16)


def _mix_kernel(x_ref, g_ref, win_ref, ca_ref, cb_ref, lng_ref, lnb_ref, pm_ref, ps_ref,
                qg_ref, kg_ref, bd_ref, wa_ref, wb_ref, wc_ref,
                h_ref, ya_ref, yb_ref, yc_ref, q_ref, k_ref, v_ref, wa_out, wb_out, wc_out,
                p_ref, ext_ref, sh_ref, cv_ref):
    i = pl.program_id(0)
    tm = x_ref.shape[0]
    w = WIDTH
    _cast_slabs((wa_ref, wb_ref, wc_ref), (wa_out, wb_out, wc_out))

    @pl.when(i == 0)
    def _():
        ext_ref[:, 0:HALO, :] = jnp.zeros((3, HALO, w), F32)

    @pl.when(i > 0)
    def _():
        ext_ref[:, 0:HALO, :] = ext_ref[:, tm:tm + HALO, :]

    h_ref[...] = _rms(x_ref[...], g_ref[...]).astype(BF16)
    for s in range(N_SLOTS):
        cols = slice(s * w, (s + 1) * w)
        p_ref[:, cols] = jnp.dot(h_ref[...], win_ref[:, cols], preferred_element_type=F32)

    def slot(s, rows=slice(None), lanes=slice(0, w)):
        return p_ref[rows, s * w + lanes.start:s * w + lanes.stop]

    ext_ref[0, HALO:HALO + tm, :] = slot(2) * slot(0)
    ext_ref[1, HALO:HALO + tm, :] = slot(3) * jax.nn.sigmoid(slot(4))
    ext_ref[2, HALO:HALO + tm, :] = slot(5)

    n_sh = HALO + tm - SUBLANES
    for r in range(1, SUBLANES):
        sh_ref[r - 1] = ext_ref[1, r:r + n_sh, :]

    def conv_a(r0, lanes):
        acc = None
        for k in range(SC_TAPS):
            start = HALO + r0 - (SC_TAPS - 1) + k
            term = ca_ref[k:k + 1, lanes] * ext_ref[0, start:start + MIX_RC, lanes]
            acc = term if acc is None else acc + term
        return acc

    def conv_b(r0, lanes):
        acc = None
        for k in range(CONF_TAPS):
            start = HALO + r0 - (CONF_TAPS - 1) + k
            r = start % SUBLANES
            base = start - r
            if r == 0:
                u = ext_ref[1, base:base + MIX_RC, lanes]
            else:
                u = sh_ref[r - 1, base:base + MIX_RC, lanes]
            term = cb_ref[k:k + 1, lanes] * u
            acc = term if acc is None else acc + term
        return acc

    row = lax.broadcasted_iota(jnp.int32, (MIX_RC, LANES), 0)
    for r0 in range(0, tm, MIX_RC):
        rows = slice(r0, r0 + MIX_RC)
        for g, win in enumerate(POOL_WINDOWS):
            lanes = slice(g * LANES, (g + 1) * LANES)
            ya_ref[rows, lanes] = (slot(1, rows, lanes) * conv_a(r0, lanes)).astype(BF16)
            cv_ref[rows, lanes] = conv_b(r0, lanes)
            u = ext_ref[2, HALO + r0:HALO + r0 + MIX_RC, lanes]
            tot = u
            for back in range(1, win):
                tot = tot + ext_ref[2, HALO + r0 - back:HALO + r0 - back + MIX_RC, lanes]
            pos = i * tm + r0 + row
            cnt = jnp.minimum(pos + 1, win).astype(F32)
            pooled = tot / cnt - u
            yc = jnp.dot(pooled.astype(BF16), pm_ref[g], preferred_element_type=F32)
            yc_ref[rows, lanes] = (yc * ps_ref[:, lanes]).astype(BF16)

    cv = cv_ref[...]
    mu = jnp.mean(cv, axis=-1, keepdims=True)
    xc = cv - mu
    ln = xc * lax.rsqrt(jnp.mean(xc * xc, axis=-1, keepdims=True) + EPS)
    ln = ln * lng_ref[...] + lnb_ref[...]
    yb_ref[...] = (ln * jax.nn.sigmoid(ln)).astype(BF16)

    def head_rms(x, gain):
        sq = x * x
        hi = sq.astype(BF16)
        lo = (sq - hi.astype(F32)).astype(BF16)
        ss = (jnp.dot(hi, bd_ref[...], preferred_element_type=F32)
              + jnp.dot(lo, bd_ref[...], preferred_element_type=F32))
        return x * lax.rsqrt(ss * (1.0 / HEAD_DIM) + EPS) * gain

    q_ref[...] = (head_rms(slot(6), qg_ref[...]) * (LOG2E * HEAD_DIM ** -0.5)).astype(BF16)
    k_ref[...] = head_rms(slot(7), kg_ref[...]).astype(BF16)
    v_ref[...] = slot(8).astype(BF16)


def _mix(x, g, w_in, conv_a, conv_b, ln_g, ln_b, pool_map, pool_scale, q_gain, k_gain, blockdiag, l,
         cast_ws):
    s, d = x.shape
    w = WIDTH
    tm = MIX_TM
    cast_in, cast_out, cast_shapes = _cast_specs(cast_ws, l, s // tm)
    layer3 = lambda i: (l, 0, 0)
    row_blk = lambda i: (i, 0)
    out_w = jax.ShapeDtypeStruct((s, w), BF16)
    return pl.pallas_call(
        _mix_kernel,
        grid=(s // tm,),
        in_specs=[
            pl.BlockSpec((tm, d), row_blk),
            pl.BlockSpec((None, 1, d), layer3),
            pl.BlockSpec((None, d, N_SLOTS * w), layer3, pipeline_mode=pl.Buffered(1)),
            pl.BlockSpec((None, SC_TAPS, w), layer3),
            pl.BlockSpec((None, CONF_TAPS, w), layer3),
            pl.BlockSpec((None, 1, w), layer3),
            pl.BlockSpec((None, 1, w), layer3),
            pl.BlockSpec((None, len(POOL_WINDOWS), POOL_GROUP, POOL_GROUP), lambda i: (l, 0, 0, 0)),
            pl.BlockSpec((None, 1, w), layer3),
            pl.BlockSpec((None, 1, w), layer3),
            pl.BlockSpec((None, 1, w), layer3),
            pl.BlockSpec((w, w), lambda i: (0, 0)),
        ] + cast_in,
        out_specs=[pl.BlockSpec((tm, d), row_blk)] + [pl.BlockSpec((tm, w), row_blk)] * 6 + cast_out,
        out_shape=[jax.ShapeDtypeStruct((s, d), BF16)] + [out_w] * 6 + cast_shapes,
        scratch_shapes=[
            pltpu.VMEM((tm, N_SLOTS * w), F32),
            pltpu.VMEM((3, HALO + tm, w), F32),
            pltpu.VMEM((SUBLANES - 1, HALO + tm - SUBLANES, w), F32),
            pltpu.VMEM((tm, w), F32),
        ],
        compiler_params=pltpu.CompilerParams(
            dimension_semantics=("arbitrary",), vmem_limit_bytes=VMEM_LIMIT),
        name="mix",
    )(x, g, w_in, conv_a, conv_b, ln_g, ln_b, pool_map, pool_scale, q_gain, k_gain, blockdiag,
      *cast_ws)


def _attn_kernel(*refs, n_cast):
    q_ref, k_ref, v_ref, u_ref = refs[:4]
    o_ref = refs[4 + n_cast]
    qm_ref, acc_ref, carry_ref, gain_ref, cost_ref, prob_ref = refs[5 + 2 * n_cast:]
    _cast_slabs(refs[4:4 + n_cast], refs[5 + n_cast:5 + 2 * n_cast])
    qi = pl.program_id(0)
    tq, tk = ATT_TQ, ATT_TK
    n_tiles = WIDTH // LANES
    hpt = HEADS_PER_TILE

    lane = lax.broadcasted_iota(jnp.int32, (tq, LANES), 1)
    for t in range(n_tiles):
        qt = q_ref[:, t * LANES:(t + 1) * LANES]
        for hh in range(hpt):
            mine = (lane >= hh * HEAD_DIM) & (lane < (hh + 1) * HEAD_DIM)
            qm_ref[t, hh * tq:(hh + 1) * tq, :] = jnp.where(mine, qt, jnp.zeros_like(qt))
    acc_ref[...] = jnp.zeros_like(acc_ref)
    carry_ref[...] = jnp.zeros_like(carry_ref)

    def step(kjs, diagonal):
        k0s = [pl.multiple_of(kj * tk, tk) for kj in kjs]
        if diagonal:
            r = lax.broadcasted_iota(jnp.int32, (tq, tk), 0) + qi * tq
            c = lax.broadcasted_iota(jnp.int32, (tq, tk), 1) + k0s[0]
            mask = jnp.concatenate([c < r] * hpt, axis=0)
        units = [(kk, t) for kk in range(len(kjs)) for t in range(n_tiles)]

        def scores(u):
            kk, t = units[u]
            return lax.dot_general(qm_ref[t], k_ref[pl.ds(k0s[kk], tk), t * LANES:(t + 1) * LANES],
                                   (((1,), (1,)), ((), ())), preferred_element_type=F32)

        def costs(u, z):
            lg = jnp.log2(1.0 + jnp.exp2(-jnp.abs(z)))
            cost = jnp.maximum(z, 0.0) + lg
            gain_ref[u] = z - cost
            if diagonal:
                cost = jnp.where(mask, cost, 0.0)
            cost_ref[u] = cost.astype(BF16)
            tot = jnp.sum(cost, axis=1, keepdims=True)
            return tot, jnp.dot(cost_ref[u], u_ref[...], preferred_element_type=F32)

        def weights(u, tot, within):
            kk, t = units[u]
            carry = carry_ref[t]
            a = jnp.exp2(gain_ref[u] - (within + carry))
            if diagonal:
                a = jnp.where(mask, a, 0.0)
            carry_ref[t] = carry + tot
            prob_ref[u] = a.astype(BF16)
            pv = jnp.dot(prob_ref[u], v_ref[pl.ds(k0s[kk], tk), t * LANES:(t + 1) * LANES],
                         preferred_element_type=F32)
            out = pv[(hpt - 1) * tq:hpt * tq]
            for hh in range(hpt - 2, -1, -1):
                out = jnp.where(lane < (hh + 1) * HEAD_DIM, pv[hh * tq:(hh + 1) * tq], out)
            acc_ref[t] += out

        zs, mids = {}, {}
        for s in range(len(units) + 2 * ATT_SKEW):
            if s < len(units):
                zs[s] = scores(s)
            if 0 <= s - ATT_SKEW < len(units):
                mids[s - ATT_SKEW] = costs(s - ATT_SKEW, zs.pop(s - ATT_SKEW))
            if 0 <= s - 2 * ATT_SKEW < len(units):
                weights(s - 2 * ATT_SKEW, *mids.pop(s - 2 * ATT_SKEW))

    diag = ((qi + 1) * tq - 1) // tk
    step([diag], True)
    n_single = diag % ATT_UNROLL

    def single(n, c):
        step([diag - 1 - n], False)
        return c
    lax.fori_loop(0, n_single, single, 0)

    def group(n, c):
        top = diag - 1 - n_single - n * ATT_UNROLL
        step([top - kk for kk in range(ATT_UNROLL)], False)
        return c
    lax.fori_loop(0, diag // ATT_UNROLL, group, 0)

    for t in range(n_tiles):
        o_ref[:, t * LANES:(t + 1) * LANES] = acc_ref[t].astype(BF16)


def _attn(q, k, v, tri, cast_ws=(), l_cast=0):
    s, w = q.shape
    tq, tk = ATT_TQ, ATT_TK
    nt, hpt = w // LANES, HEADS_PER_TILE
    cast_in, cast_out, cast_shapes = _cast_specs(cast_ws, l_cast, s // tq)
    resident = dict(pipeline_mode=pl.Buffered(1))
    return pl.pallas_call(
        functools.partial(_attn_kernel, n_cast=len(cast_ws)),
        grid=(s // tq,),
        in_specs=[
            pl.BlockSpec((tq, w), lambda i: (i, 0)),
            pl.BlockSpec((s, w), lambda i: (0, 0), **resident),
            pl.BlockSpec((s, w), lambda i: (0, 0), **resident),
            pl.BlockSpec((tk, tk), lambda i: (0, 0), **resident),
        ] + cast_in,
        out_specs=[pl.BlockSpec((tq, w), lambda i: (i, 0))] + cast_out,
        out_shape=[jax.ShapeDtypeStruct((s, w), BF16)] + cast_shapes,
        scratch_shapes=[
            pltpu.VMEM((nt, hpt * tq, LANES), BF16),
            pltpu.VMEM((nt, tq, LANES), F32),
            pltpu.VMEM((nt, hpt * tq, 1), F32),
            pltpu.VMEM((ATT_UNROLL * nt, hpt * tq, tk), F32),
            pltpu.VMEM((ATT_UNROLL * nt, hpt * tq, tk), BF16),
            pltpu.VMEM((ATT_UNROLL * nt, hpt * tq, tk), BF16),
        ],
        compiler_params=pltpu.CompilerParams(
            dimension_semantics=("arbitrary",), vmem_limit_bytes=VMEM_LIMIT),
        name="attn",
    )(q, k, v, tri, *cast_ws)


def _gate_kernel(h_ref, ya_ref, yb_ref, yc_ref, yd_ref, wg_ref, wb_ref, o_ref, acc_ref):
    b = pl.program_id(2)
    gate = jax.nn.sigmoid(jnp.dot(h_ref[...], wg_ref[...].astype(BF16),
                                  preferred_element_type=F32))
    y = yd_ref[...]
    for n, y_ref in reversed(list(enumerate((ya_ref, yb_ref, yc_ref)))):
        y = jnp.where(b == n, y_ref[...], y)
    term = gate * jnp.dot(y, wb_ref[...].astype(BF16), preferred_element_type=F32)

    @pl.when((pl.program_id(0) == 0) & (pl.program_id(1) == 0) & (b == 0))
    def _():
        acc_ref[...] = jnp.zeros_like(acc_ref)

    acc = jnp.where(b == 0, term, acc_ref[...] + term)
    acc_ref[...] = acc
    o_ref[...] = acc.astype(BF16)


def _gate(h, ys, w_gate, w_branch, l):
    s, d = h.shape
    w = WIDTH
    tm, tn = GATE_TM, GATE_TN
    nn = d // tn
    y_spec = pl.BlockSpec((tm, w), lambda i, n, b: (i, 0))
    return pl.pallas_call(
        _gate_kernel,
        grid=(s // tm, nn, N_BRANCH),
        in_specs=[
            pl.BlockSpec((tm, d), lambda i, n, b: (i, 0)),
            y_spec, y_spec, y_spec, y_spec,
            pl.BlockSpec((None, d, tn), lambda i, n, b: (l, 0, b * nn + n)),
            pl.BlockSpec((None, None, w, tn), lambda i, n, b: (l, b, 0, n)),
        ],
        out_specs=pl.BlockSpec((tm, tn), lambda i, n, b: (i, n)),
        out_shape=jax.ShapeDtypeStruct((s, d), BF16),
        scratch_shapes=[pltpu.VMEM((tm, tn), F32)],
        compiler_params=pltpu.CompilerParams(
            dimension_semantics=("arbitrary", "arbitrary", "arbitrary"),
            vmem_limit_bytes=VMEM_LIMIT),
        name="gate",
    )(h, *ys, w_gate, w_branch)


def _proj_kernel(x_ref, m_ref, w_ref, o_ref):
    o_ref[...] = x_ref[...] + jnp.dot(m_ref[...], w_ref[...].astype(BF16),
                                      preferred_element_type=F32)


def _proj(x, merged, w_out, l):
    s, d = x.shape
    tm = PROJ_TM
    return pl.pallas_call(
        _proj_kernel,
        grid=(s // tm,),
        in_specs=[
            pl.BlockSpec((tm, d), lambda i: (i, 0)),
            pl.BlockSpec((tm, d), lambda i: (i, 0)),
            pl.BlockSpec((None, d, d), lambda i: (l, 0, 0), pipeline_mode=pl.Buffered(1)),
        ],
        out_specs=pl.BlockSpec((tm, d), lambda i: (i, 0)),
        out_shape=jax.ShapeDtypeStruct((s, d), F32),
        compiler_params=pltpu.CompilerParams(
            dimension_semantics=("arbitrary",), vmem_limit_bytes=VMEM_LIMIT),
        name="proj",
    )(x, merged, w_out)


def _constants():
    j = lax.broadcasted_iota(jnp.int32, (ATT_TK, ATT_TK), 0)
    c = lax.broadcasted_iota(jnp.int32, (ATT_TK, ATT_TK), 1)
    tri = (j > c).astype(BF16)
    hr = lax.broadcasted_iota(jnp.int32, (WIDTH, WIDTH), 0) // HEAD_DIM
    hc = lax.broadcasted_iota(jnp.int32, (WIDTH, WIDTH), 1) // HEAD_DIM
    blockdiag = (hr == hc).astype(BF16)
    return tri, blockdiag


def kernel(x, ffn1_norm, ffn1_w1, ffn1_w3, ffn1_w2, mix_norm, w_in, conv_a, conv_b, ln_b_gain,
           ln_b_bias, pool_map, pool_scale, q_norm, k_norm, w_branch, w_gate, w_out, ffn2_norm,
           ffn2_w1, ffn2_w3, ffn2_w2):
    b, s, d = x.shape
    depth = w_in.shape[0]
    tri, blockdiag = _constants()
    bf = lambda a: a.astype(BF16)
    row = lambda a: a.reshape(depth, 1, -1)
    win, pmap = bf(w_in), bf(pool_map)
    f1g, f2g, mg = row(ffn1_norm), row(ffn2_norm), row(mix_norm)
    lng, lnb, psc = row(ln_b_gain), row(ln_b_bias), row(pool_scale)
    qg, kg = row(jnp.tile(q_norm, (1, HEADS))), row(jnp.tile(k_norm, (1, HEADS)))
    ffn1_ws, ffn2_ws = (ffn1_w1, ffn1_w3, ffn1_w2), (ffn2_w1, ffn2_w3, ffn2_w2)
    outs = []
    for bi in range(b):
        xb = x[bi]
        f1 = [bf(wt[0]) for wt in ffn1_ws]
        for l in range(depth):
            xb = _ffn(xb, f1g, l, *f1)
            h, ya, yb, yc, q, k, v, *f2 = _mix(xb, mg, win, conv_a, conv_b, lng, lnb, pmap, psc,
                                               qg, kg, blockdiag, l, ffn2_ws)
            if l + 1 < depth:
                yd, *f1 = _attn(q, k, v, tri, ffn1_ws, l + 1)
            else:
                yd, = _attn(q, k, v, tri)
            merged = _gate(h, (ya, yb, yc, yd), w_gate, w_branch, l)
            xb = _proj(xb, merged, w_out, l)
            xb = _ffn(xb, f2g, l, *f2)
        outs.append(xb)
    return jnp.stack(outs, axis=0)
```

```python
import functools

import jax
import jax.numpy as jnp
from jax import lax
from jax.experimental import pallas as pl
from jax.experimental.pallas import tpu as pltpu

F32 = jnp.float32
BF16 = jnp.bfloat16

EPS = 1e-6
LOG2E = 1.4426950408889634
N_BRANCH = 4
WIDTH = 512
N_SLOTS = 9
SC_TAPS = 3
CONF_TAPS = 31
POOL_WINDOWS = (2, 4, 8, 16)
POOL_GROUP = WIDTH // len(POOL_WINDOWS)
HEADS = 8
HEAD_DIM = WIDTH // HEADS
LANES = 128
SUBLANES = 8
HEADS_PER_TILE = LANES // HEAD_DIM

FFN_TM = 1024
FFN_TF = 512
FFN_RC = 256
MIX_TM = 256
HALO = 32
MIX_RC = 64
ATT_TQ = 256
ATT_TK = 256
ATT_SKEW = 3
ATT_UNROLL = 2
GATE_TM = 1024
GATE_TN = 1024
GATE_RC = 256
PROJ_TM = 512
VMEM_LIMIT = 56 * 1024 * 1024


def _rms(x, g):
    return x * lax.rsqrt(jnp.mean(x * x, axis=-1, keepdims=True) + EPS) * g


def _ffn_kernel(x_ref, g_ref, w1_ref, w3_ref, w2_ref, o_ref, h_ref):
    j = pl.program_id(1)
    tm = x_ref.shape[0]

    @pl.when(j == 0)
    def _():
        def body(c, carry):
            r = pl.multiple_of(c * FFN_RC, FFN_RC)
            xs = x_ref[pl.ds(r, FFN_RC), :]
            h_ref[pl.ds(r, FFN_RC), :] = _rms(xs, g_ref[...]).astype(BF16)
            o_ref[pl.ds(r, FFN_RC), :] = xs
            return carry
        lax.fori_loop(0, tm // FFN_RC, body, 0)

    n_chunks = tm // FFN_RC

    def up(c):
        h = h_ref[pl.ds(c * FFN_RC, FFN_RC), :]
        return (jnp.dot(h, w1_ref[...], preferred_element_type=F32),
                jnp.dot(h, w3_ref[...], preferred_element_type=F32))

    def down(c, a, b):
        act = (a * jax.nn.sigmoid(a)) * b * 0.5
        o_ref[pl.ds(c * FFN_RC, FFN_RC), :] += jnp.dot(act.astype(BF16), w2_ref[...],
                                                      preferred_element_type=F32)

    pending = up(0)
    for c in range(n_chunks):
        nxt = up(c + 1) if c + 1 < n_chunks else None
        down(c, *pending)
        pending = nxt


def _ffn(x, g, l, w1, w3, w2):
    s, d = x.shape
    f = w1.shape[1]
    return pl.pallas_call(
        _ffn_kernel,
        grid=(s // FFN_TM, f // FFN_TF),
        in_specs=[
            pl.BlockSpec((FFN_TM, d), lambda i, j: (i, 0)),
            pl.BlockSpec((None, 1, d), lambda i, j: (l, 0, 0)),
            pl.BlockSpec((d, FFN_TF), lambda i, j: (0, j)),
            pl.BlockSpec((d, FFN_TF), lambda i, j: (0, j)),
            pl.BlockSpec((FFN_TF, d), lambda i, j: (j, 0)),
        ],
        out_specs=pl.BlockSpec((FFN_TM, d), lambda i, j: (i, 0)),
        out_shape=jax.ShapeDtypeStruct((s, d), F32),
        scratch_shapes=[pltpu.VMEM((FFN_TM, d), BF16)],
        compiler_params=pltpu.CompilerParams(
            dimension_semantics=("arbitrary", "arbitrary"), vmem_limit_bytes=VMEM_LIMIT),
        name="ffn",
    )(x, g, w1, w3, w2)


def _cast_specs(ws, l, n_steps):
    in_specs, out_specs, out_shapes = [], [], []
    for w in ws:
        _, r, c = w.shape
        rows = r // n_steps
        assert rows * n_steps == r and rows % 16 == 0, (r, n_steps)
        in_specs.append(pl.BlockSpec((None, rows, c), lambda i: (l, i, 0)))
        out_specs.append(pl.BlockSpec((rows, c), lambda i: (i, 0)))
        out_shapes.append(jax.ShapeDtypeStruct((r, c), BF16))
    return in_specs, out_specs, out_shapes


def _cast_slabs(in_refs, out_refs):
    for src, dst in zip(in_refs, out_refs):
        dst[...] = src[...].astype(BF16)


def _mix_kernel(x_ref, g_ref, win_ref, ca_ref, cb_ref, lng_ref, lnb_ref, pm_ref, ps_ref,
                qg_ref, kg_ref, bd_ref, wa_ref, wb_ref, wc_ref,
                h_ref, ya_ref, yb_ref, yc_ref, q_ref, k_ref, v_ref, wa_out, wb_out, wc_out,
                p_ref, ext_ref, sh_ref, cv_ref):
    i = pl.program_id(0)
    tm = x_ref.shape[0]
    w = WIDTH
    _cast_slabs((wa_ref, wb_ref, wc_ref), (wa_out, wb_out, wc_out))

    @pl.when(i == 0)
    def _():
        ext_ref[:, 0:HALO, :] = jnp.zeros((3, HALO, w), F32)

    @pl.when(i > 0)
    def _():
        ext_ref[:, 0:HALO, :] = ext_ref[:, tm:tm + HALO, :]

    h_ref[...] = _rms(x_ref[...], g_ref[...]).astype(BF16)
    for s in range(N_SLOTS):
        cols = slice(s * w, (s + 1) * w)
        p_ref[:, cols] = jnp.dot(h_ref[...], win_ref[:, cols], preferred_element_type=F32)

    def slot(s, rows=slice(None), lanes=slice(0, w)):
        return p_ref[rows, s * w + lanes.start:s * w + lanes.stop]

    ext_ref[0, HALO:HALO + tm, :] = slot(2) * slot(0)
    ext_ref[1, HALO:HALO + tm, :] = slot(3) * jax.nn.sigmoid(slot(4))
    ext_ref[2, HALO:HALO + tm, :] = slot(5)

    n_sh = HALO + tm - SUBLANES
    for r in range(1, SUBLANES):
        sh_ref[r - 1] = ext_ref[1, r:r + n_sh, :]

    def conv_a(r0, lanes):
        acc = None
        for k in range(SC_TAPS):
            start = HALO + r0 - (SC_TAPS - 1) + k
            term = ca_ref[k:k + 1, lanes] * ext_ref[0, start:start + MIX_RC, lanes]
            acc = term if acc is None else acc + term
        return acc

    def conv_b(r0, lanes):
        acc = None
        for k in range(CONF_TAPS):
            start = HALO + r0 - (CONF_TAPS - 1) + k
            r = start % SUBLANES
            base = start - r
            if r == 0:
                u = ext_ref[1, base:base + MIX_RC, lanes]
            else:
                u = sh_ref[r - 1, base:base + MIX_RC, lanes]
            term = cb_ref[k:k + 1, lanes] * u
            acc = term if acc is None else acc + term
        return acc

    row = lax.broadcasted_iota(jnp.int32, (MIX_RC, LANES), 0)
    for r0 in range(0, tm, MIX_RC):
        rows = slice(r0, r0 + MIX_RC)
        for g, win in enumerate(POOL_WINDOWS):
            lanes = slice(g * LANES, (g + 1) * LANES)
            ya_ref[rows, lanes] = (slot(1, rows, lanes) * conv_a(r0, lanes)).astype(BF16)
            cv_ref[rows, lanes] = conv_b(r0, lanes)
            u = ext_ref[2, HALO + r0:HALO + r0 + MIX_RC, lanes]
            tot = u
            for back in range(1, win):
                tot = tot + ext_ref[2, HALO + r0 - back:HALO + r0 - back + MIX_RC, lanes]
            pos = i * tm + r0 + row
            cnt = jnp.minimum(pos + 1, win).astype(F32)
            pooled = tot / cnt - u
            yc = jnp.dot(pooled.astype(BF16), pm_ref[g], preferred_element_type=F32)
            yc_ref[rows, lanes] = (yc * ps_ref[:, lanes]).astype(BF16)

    cv = cv_ref[...]
    mu = jnp.mean(cv, axis=-1, keepdims=True)
    xc = cv - mu
    ln = xc * lax.rsqrt(jnp.mean(xc * xc, axis=-1, keepdims=True) + EPS)
    ln = ln * lng_ref[...] + lnb_ref[...]
    yb_ref[...] = (ln * jax.nn.sigmoid(ln)).astype(BF16)

    def head_rms(x, gain):
        sq = x * x
        hi = sq.astype(BF16)
        lo = (sq - hi.astype(F32)).astype(BF16)
        ss = (jnp.dot(hi, bd_ref[...], preferred_element_type=F32)
              + jnp.dot(lo, bd_ref[...], preferred_element_type=F32))
        return x * lax.rsqrt(ss * (1.0 / HEAD_DIM) + EPS) * gain

    q_ref[...] = (head_rms(slot(6), qg_ref[...]) * (LOG2E * HEAD_DIM ** -0.5)).astype(BF16)
    k_ref[...] = head_rms(slot(7), kg_ref[...]).astype(BF16)
    v_ref[...] = slot(8).astype(BF16)


def _mix(x, g, w_in, conv_a, conv_b, ln_g, ln_b, pool_map, pool_scale, q_gain, k_gain, blockdiag, l,
         cast_ws):
    s, d = x.shape
    w = WIDTH
    tm = MIX_TM
    cast_in, cast_out, cast_shapes = _cast_specs(cast_ws, l, s // tm)
    layer3 = lambda i: (l, 0, 0)
    row_blk = lambda i: (i, 0)
    out_w = jax.ShapeDtypeStruct((s, w), BF16)
    return pl.pallas_call(
        _mix_kernel,
        grid=(s // tm,),
        in_specs=[
            pl.BlockSpec((tm, d), row_blk),
            pl.BlockSpec((None, 1, d), layer3),
            pl.BlockSpec((None, d, N_SLOTS * w), layer3, pipeline_mode=pl.Buffered(1)),
            pl.BlockSpec((None, SC_TAPS, w), layer3),
            pl.BlockSpec((None, CONF_TAPS, w), layer3),
            pl.BlockSpec((None, 1, w), layer3),
            pl.BlockSpec((None, 1, w), layer3),
            pl.BlockSpec((None, len(POOL_WINDOWS), POOL_GROUP, POOL_GROUP), lambda i: (l, 0, 0, 0)),
            pl.BlockSpec((None, 1, w), layer3),
            pl.BlockSpec((None, 1, w), layer3),
            pl.BlockSpec((None, 1, w), layer3),
            pl.BlockSpec((w, w), lambda i: (0, 0)),
        ] + cast_in,
        out_specs=[pl.BlockSpec((tm, d), row_blk)] + [pl.BlockSpec((tm, w), row_blk)] * 6 + cast_out,
        out_shape=[jax.ShapeDtypeStruct((s, d), BF16)] + [out_w] * 6 + cast_shapes,
        scratch_shapes=[
            pltpu.VMEM((tm, N_SLOTS * w), F32),
            pltpu.VMEM((3, HALO + tm, w), F32),
            pltpu.VMEM((SUBLANES - 1, HALO + tm - SUBLANES, w), F32),
            pltpu.VMEM((tm, w), F32),
        ],
        compiler_params=pltpu.CompilerParams(
            dimension_semantics=("arbitrary",), vmem_limit_bytes=VMEM_LIMIT),
        name="mix",
    )(x, g, w_in, conv_a, conv_b, ln_g, ln_b, pool_map, pool_scale, q_gain, k_gain, blockdiag,
      *cast_ws)


def _attn_kernel(*refs, n_cast):
    q_ref, k_ref, v_ref, u_ref = refs[:4]
    o_ref = refs[4 + n_cast]
    qm_ref, acc_ref, carry_ref, gain_ref, cost_ref, prob_ref = refs[5 + 2 * n_cast:]
    _cast_slabs(refs[4:4 + n_cast], refs[5 + n_cast:5 + 2 * n_cast])
    qi = pl.program_id(0)
    tq, tk = ATT_TQ, ATT_TK
    n_tiles = WIDTH // LANES
    hpt = HEADS_PER_TILE

    lane = lax.broadcasted_iota(jnp.int32, (tq, LANES), 1)
    for t in range(n_tiles):
        qt = q_ref[:, t * LANES:(t + 1) * LANES]
        for hh in range(hpt):
            mine = (lane >= hh * HEAD_DIM) & (lane < (hh + 1) * HEAD_DIM)
            qm_ref[t, hh * tq:(hh + 1) * tq, :] = jnp.where(mine, qt, jnp.zeros_like(qt))
    acc_ref[...] = jnp.zeros_like(acc_ref)
    carry_ref[...] = jnp.zeros_like(carry_ref)

    def step(kjs, diagonal):
        k0s = [pl.multiple_of(kj * tk, tk) for kj in kjs]
        if diagonal:
            r = lax.broadcasted_iota(jnp.int32, (tq, tk), 0) + qi * tq
            c = lax.broadcasted_iota(jnp.int32, (tq, tk), 1) + k0s[0]
            mask = jnp.concatenate([c < r] * hpt, axis=0)
        units = [(kk, t) for kk in range(len(kjs)) for t in range(n_tiles)]

        def scores(u):
            kk, t = units[u]
            return lax.dot_general(qm_ref[t], k_ref[pl.ds(k0s[kk], tk), t * LANES:(t + 1) * LANES],
                                   (((1,), (1,)), ((), ())), preferred_element_type=F32)

        def costs(u, z):
            lg = jnp.log2(1.0 + jnp.exp2(-jnp.abs(z)))
            cost = jnp.maximum(z, 0.0) + lg
            gain_ref[u] = z - cost
            if diagonal:
                cost = jnp.where(mask, cost, 0.0)
            cost_ref[u] = cost.astype(BF16)
            tot = jnp.sum(cost, axis=1, keepdims=True)
            return tot, jnp.dot(cost_ref[u], u_ref[...], preferred_element_type=F32)

        def weights(u, tot, within):
            kk, t = units[u]
            carry = carry_ref[t]
            a = jnp.exp2(gain_ref[u] - (within + carry))
            if diagonal:
                a = jnp.where(mask, a, 0.0)
            carry_ref[t] = carry + tot
            prob_ref[u] = a.astype(BF16)
            pv = jnp.dot(prob_ref[u], v_ref[pl.ds(k0s[kk], tk), t * LANES:(t + 1) * LANES],
                         preferred_element_type=F32)
            out = pv[(hpt - 1) * tq:hpt * tq]
            for hh in range(hpt - 2, -1, -1):
                out = jnp.where(lane < (hh + 1) * HEAD_DIM, pv[hh * tq:(hh + 1) * tq], out)
            acc_ref[t] += out

        zs, mids = {}, {}
        for s in range(len(units) + 2 * ATT_SKEW):
            if s < len(units):
                zs[s] = scores(s)
            if 0 <= s - ATT_SKEW < len(units):
                mids[s - ATT_SKEW] = costs(s - ATT_SKEW, zs.pop(s - ATT_SKEW))
            if 0 <= s - 2 * ATT_SKEW < len(units):
                weights(s - 2 * ATT_SKEW, *mids.pop(s - 2 * ATT_SKEW))

    diag = ((qi + 1) * tq - 1) // tk
    step([diag], True)
    n_single = diag % ATT_UNROLL

    def single(n, c):
        step([diag - 1 - n], False)
        return c
    lax.fori_loop(0, n_single, single, 0)

    def group(n, c):
        top = diag - 1 - n_single - n * ATT_UNROLL
        step([top - kk for kk in range(ATT_UNROLL)], False)
        return c
    lax.fori_loop(0, diag // ATT_UNROLL, group, 0)

    for t in range(n_tiles):
        o_ref[:, t * LANES:(t + 1) * LANES] = acc_ref[t].astype(BF16)


def _attn(q, k, v, tri, cast_ws=(), l_cast=0):
    s, w = q.shape
    tq, tk = ATT_TQ, ATT_TK
    nt, hpt = w // LANES, HEADS_PER_TILE
    cast_in, cast_out, cast_shapes = _cast_specs(cast_ws, l_cast, s // tq)
    resident = dict(pipeline_mode=pl.Buffered(1))
    return pl.pallas_call(
        functools.partial(_attn_kernel, n_cast=len(cast_ws)),
        grid=(s // tq,),
        in_specs=[
            pl.BlockSpec((tq, w), lambda i: (i, 0)),
            pl.BlockSpec((s, w), lambda i: (0, 0), **resident),
            pl.BlockSpec((s, w), lambda i: (0, 0), **resident),
            pl.BlockSpec((tk, tk), lambda i: (0, 0), **resident),
        ] + cast_in,
        out_specs=[pl.BlockSpec((tq, w), lambda i: (i, 0))] + cast_out,
        out_shape=[jax.ShapeDtypeStruct((s, w), BF16)] + cast_shapes,
        scratch_shapes=[
            pltpu.VMEM((nt, hpt * tq, LANES), BF16),
            pltpu.VMEM((nt, tq, LANES), F32),
            pltpu.VMEM((nt, hpt * tq, 1), F32),
            pltpu.VMEM((ATT_UNROLL * nt, hpt * tq, tk), F32),
            pltpu.VMEM((ATT_UNROLL * nt, hpt * tq, tk), BF16),
            pltpu.VMEM((ATT_UNROLL * nt, hpt * tq, tk), BF16),
        ],
        compiler_params=pltpu.CompilerParams(
            dimension_semantics=("arbitrary",), vmem_limit_bytes=VMEM_LIMIT),
        name="attn",
    )(q, k, v, tri, *cast_ws)


def _gate_kernel(h_ref, ya_ref, yb_ref, yc_ref, yd_ref, wg_ref, wb_ref, o_ref, acc_ref):
    b = pl.program_id(2)

    @pl.when((pl.program_id(0) == 0) & (pl.program_id(1) == 0) & (b == 0))
    def _():
        acc_ref[...] = jnp.zeros_like(acc_ref)

    wg = wg_ref[...].astype(BF16)
    wb = wb_ref[...].astype(BF16)
    for c in range(h_ref.shape[0] // GATE_RC):
        rows = pl.ds(c * GATE_RC, GATE_RC)
        gate = jax.nn.sigmoid(jnp.dot(h_ref[rows, :], wg, preferred_element_type=F32))
        y = yd_ref[rows, :]
        for n, y_ref in reversed(list(enumerate((ya_ref, yb_ref, yc_ref)))):
            y = jnp.where(b == n, y_ref[rows, :], y)
        term = gate * jnp.dot(y, wb, preferred_element_type=F32)
        acc = jnp.where(b == 0, term, acc_ref[rows, :] + term)
        acc_ref[rows, :] = acc
        o_ref[rows, :] = acc.astype(BF16)


def _gate(h, ys, w_gate, w_branch, l):
    s, d = h.shape
    w = WIDTH
    tm, tn = GATE_TM, GATE_TN
    nn = d // tn
    y_spec = pl.BlockSpec((tm, w), lambda i, n, b: (i, 0))
    return pl.pallas_call(
        _gate_kernel,
        grid=(s // tm, nn, N_BRANCH),
        in_specs=[
            pl.BlockSpec((tm, d), lambda i, n, b: (i, 0)),
            y_spec, y_spec, y_spec, y_spec,
            pl.BlockSpec((None, d, tn), lambda i, n, b: (l, 0, b * nn + n)),
            pl.BlockSpec((None, None, w, tn), lambda i, n, b: (l, b, 0, n)),
        ],
        out_specs=pl.BlockSpec((tm, tn), lambda i, n, b: (i, n)),
        out_shape=jax.ShapeDtypeStruct((s, d), BF16),
        scratch_shapes=[pltpu.VMEM((tm, tn), F32)],
        compiler_params=pltpu.CompilerParams(
            dimension_semantics=("arbitrary", "arbitrary", "arbitrary"),
            vmem_limit_bytes=VMEM_LIMIT),
        name="gate",
    )(h, *ys, w_gate, w_branch)


def _proj_kernel(x_ref, m_ref, w_ref, o_ref):
    o_ref[...] = x_ref[...] + jnp.dot(m_ref[...], w_ref[...].astype(BF16),
                                      preferred_element_type=F32)


def _proj(x, merged, w_out, l):
    s, d = x.shape
    tm = PROJ_TM
    return pl.pallas_call(
        _proj_kernel,
        grid=(s // tm,),
        in_specs=[
            pl.BlockSpec((tm, d), lambda i: (i, 0)),
            pl.BlockSpec((tm, d), lambda i: (i, 0)),
            pl.BlockSpec((None, d, d), lambda i: (l, 0, 0), pipeline_mode=pl.Buffered(1)),
        ],
        out_specs=pl.BlockSpec((tm, d), lambda i: (i, 0)),
        out_shape=jax.ShapeDtypeStruct((s, d), F32),
        compiler_params=pltpu.CompilerParams(
            dimension_semantics=("arbitrary",), vmem_limit_bytes=VMEM_LIMIT),
        name="proj",
    )(x, merged, w_out)


def _constants():
    j = lax.broadcasted_iota(jnp.int32, (ATT_TK, ATT_TK), 0)
    c = lax.broadcasted_iota(jnp.int32, (ATT_TK, ATT_TK), 1)
    tri = (j > c).astype(BF16)
    hr = lax.broadcasted_iota(jnp.int32, (WIDTH, WIDTH), 0) // HEAD_DIM
    hc = lax.broadcasted_iota(jnp.int32, (WIDTH, WIDTH), 1) // HEAD_DIM
    blockdiag = (hr == hc).astype(BF16)
    return tri, blockdiag


def kernel(x, ffn1_norm, ffn1_w1, ffn1_w3, ffn1_w2, mix_norm, w_in, conv_a, conv_b, ln_b_gain,
           ln_b_bias, pool_map, pool_scale, q_norm, k_norm, w_branch, w_gate, w_out, ffn2_norm,
           ffn2_w1, ffn2_w3, ffn2_w2):
    b, s, d = x.shape
    depth = w_in.shape[0]
    tri, blockdiag = _constants()
    bf = lambda a: a.astype(BF16)
    row = lambda a: a.reshape(depth, 1, -1)
    win, pmap = bf(w_in), bf(pool_map)
    f1g, f2g, mg = row(ffn1_norm), row(ffn2_norm), row(mix_norm)
    lng, lnb, psc = row(ln_b_gain), row(ln_b_bias), row(pool_scale)
    qg, kg = row(jnp.tile(q_norm, (1, HEADS))), row(jnp.tile(k_norm, (1, HEADS)))
    ffn1_ws, ffn2_ws = (ffn1_w1, ffn1_w3, ffn1_w2), (ffn2_w1, ffn2_w3, ffn2_w2)
    outs = []
    for bi in range(b):
        xb = x[bi]
        f1 = [bf(wt[0]) for wt in ffn1_ws]
        for l in range(depth):
            xb = _ffn(xb, f1g, l, *f1)
            h, ya, yb, yc, q, k, v, *f2 = _mix(xb, mg, win, conv_a, conv_b, lng, lnb, pmap, psc,
                                               qg, kg, blockdiag, l, ffn2_ws)
            if l + 1 < depth:
                yd, *f1 = _attn(q, k, v, tri, ffn1_ws, l + 1)
            else:
                yd, = _attn(q, k, v, tri)
            merged = _gate(h, (ya, yb, yc, yd), w_gate, w_branch, l)
            xb = _proj(xb, merged, w_out, l)
            xb = _ffn(xb, f2g, l, *f2)
        outs.append(xb)
    return jnp.stack(outs, axis=0)
```

```python
import functools

import jax
import jax.numpy as jnp
from jax import lax
from jax.experimental import pallas as pl
from jax.experimental.pallas import tpu as pltpu

F32 = jnp.float32
BF16 = jnp.bfloat16

EPS = 1e-6
LOG2E = 1.4426950408889634
N_BRANCH = 4
WIDTH = 512
N_SLOTS = 9
SC_TAPS = 3
CONF_TAPS = 31
POOL_WINDOWS = (2, 4, 8, 16)
POOL_GROUP = WIDTH // len(POOL_WINDOWS)
HEADS = 8
HEAD_DIM = WIDTH // HEADS
LANES = 128
SUBLANES = 8
HEADS_PER_TILE = LANES // HEAD_DIM

FFN_TM = 1024
FFN_TF = 512
FFN_RC = 256
MIX_TM = 256
HALO = 32
MIX_RC = 64
ATT_TQ = 256
ATT_TK = 256
ATT_SKEW = 3
ATT_UNROLL = 2
GATE_TM = 1024
GATE_TN = 1024
GATE_RC = 256
PROJ_TM = 512
VMEM_LIMIT = 56 * 1024 * 1024


def _rms(x, g):
    return x * lax.rsqrt(jnp.mean(x * x, axis=-1, keepdims=True) + EPS) * g


def _ffn_kernel(x_ref, g_ref, w1_ref, w3_ref, w2_ref, o_ref, h_ref):
    j = pl.program_id(1)
    n_chunks = x_ref.shape[0] // FFN_RC

    def step(first):
        def up(c):
            rows = pl.ds(c * FFN_RC, FFN_RC)
            if first:
                h_ref[rows, :] = _rms(x_ref[rows, :], g_ref[...]).astype(BF16)
            h = h_ref[rows, :]
            return (jnp.dot(h, w1_ref[...], preferred_element_type=F32),
                    jnp.dot(h, w3_ref[...], preferred_element_type=F32))

        def down(c, a, b):
            rows = pl.ds(c * FFN_RC, FFN_RC)
            act = (a * jax.nn.sigmoid(a)) * b * 0.5
            base = x_ref[rows, :] if first else o_ref[rows, :]
            o_ref[rows, :] = base + jnp.dot(act.astype(BF16), w2_ref[...],
                                            preferred_element_type=F32)

        pending = up(0)
        for c in range(n_chunks):
            nxt = up(c + 1) if c + 1 < n_chunks else None
            down(c, *pending)
            pending = nxt

    pl.when(j == 0)(functools.partial(step, True))
    pl.when(j > 0)(functools.partial(step, False))


def _ffn(x, g, l, w1, w3, w2):
    s, d = x.shape
    f = w1.shape[1]
    return pl.pallas_call(
        _ffn_kernel,
        grid=(s // FFN_TM, f // FFN_TF),
        in_specs=[
            pl.BlockSpec((FFN_TM, d), lambda i, j: (i, 0)),
            pl.BlockSpec((None, 1, d), lambda i, j: (l, 0, 0)),
            pl.BlockSpec((d, FFN_TF), lambda i, j: (0, j)),
            pl.BlockSpec((d, FFN_TF), lambda i, j: (0, j)),
            pl.BlockSpec((FFN_TF, d), lambda i, j: (j, 0)),
        ],
        out_specs=pl.BlockSpec((FFN_TM, d), lambda i, j: (i, 0)),
        out_shape=jax.ShapeDtypeStruct((s, d), F32),
        scratch_shapes=[pltpu.VMEM((FFN_TM, d), BF16)],
        compiler_params=pltpu.CompilerParams(
            dimension_semantics=("arbitrary", "arbitrary"), vmem_limit_bytes=VMEM_LIMIT),
        name="ffn",
    )(x, g, w1, w3, w2)


def _cast_specs(ws, l, n_steps):
    in_specs, out_specs, out_shapes = [], [], []
    for w in ws:
        _, r, c = w.shape
        rows = r // n_steps
        assert rows * n_steps == r and rows % 16 == 0, (r, n_steps)
        in_specs.append(pl.BlockSpec((None, rows, c), lambda i: (l, i, 0)))
        out_specs.append(pl.BlockSpec((rows, c), lambda i: (i, 0)))
        out_shapes.append(jax.ShapeDtypeStruct((r, c), BF16))
    return in_specs, out_specs, out_shapes


def _cast_slabs(in_refs, out_refs):
    for src, dst in zip(in_refs, out_refs):
        dst[...] = src[...].astype(BF16)


def _mix_kernel(x_ref, g_ref, win_ref, ca_ref, cb_ref, lng_ref, lnb_ref, pm_ref, ps_ref,
                qg_ref, kg_ref, bd_ref, wa_ref, wb_ref, wc_ref,
                h_ref, ya_ref, yb_ref, yc_ref, q_ref, k_ref, v_ref, wa_out, wb_out, wc_out,
                p_ref, ext_ref, sh_ref, cv_ref):
    i = pl.program_id(0)
    tm = x_ref.shape[0]
    w = WIDTH
    _cast_slabs((wa_ref, wb_ref, wc_ref), (wa_out, wb_out, wc_out))

    @pl.when(i == 0)
    def _():
        ext_ref[:, 0:HALO, :] = jnp.zeros((3, HALO, w), F32)

    @pl.when(i > 0)
    def _():
        ext_ref[:, 0:HALO, :] = ext_ref[:, tm:tm + HALO, :]

    h_ref[...] = _rms(x_ref[...], g_ref[...]).astype(BF16)
    for s in range(N_SLOTS):
        cols = slice(s * w, (s + 1) * w)
        p_ref[:, cols] = jnp.dot(h_ref[...], win_ref[:, cols], preferred_element_type=F32)

    def slot(s, rows=slice(None), lanes=slice(0, w)):
        return p_ref[rows, s * w + lanes.start:s * w + lanes.stop]

    ext_ref[0, HALO:HALO + tm, :] = slot(2) * slot(0)
    ext_ref[1, HALO:HALO + tm, :] = slot(3) * jax.nn.sigmoid(slot(4))
    ext_ref[2, HALO:HALO + tm, :] = slot(5)

    n_sh = HALO + tm - SUBLANES
    for r in range(1, SUBLANES):
        sh_ref[r - 1] = ext_ref[1, r:r + n_sh, :]

    def conv_a(r0, lanes):
        acc = None
        for k in range(SC_TAPS):
            start = HALO + r0 - (SC_TAPS - 1) + k
            term = ca_ref[k:k + 1, lanes] * ext_ref[0, start:start + MIX_RC, lanes]
            acc = term if acc is None else acc + term
        return acc

    def conv_b(r0, lanes):
        acc = None
        for k in range(CONF_TAPS):
            start = HALO + r0 - (CONF_TAPS - 1) + k
            r = start % SUBLANES
            base = start - r
            if r == 0:
                u = ext_ref[1, base:base + MIX_RC, lanes]
            else:
                u = sh_ref[r - 1, base:base + MIX_RC, lanes]
            term = cb_ref[k:k + 1, lanes] * u
            acc = term if acc is None else acc + term
        return acc

    row = lax.broadcasted_iota(jnp.int32, (MIX_RC, LANES), 0)
    for r0 in range(0, tm, MIX_RC):
        rows = slice(r0, r0 + MIX_RC)
        for g, win in enumerate(POOL_WINDOWS):
            lanes = slice(g * LANES, (g + 1) * LANES)
            ya_ref[rows, lanes] = (slot(1, rows, lanes) * conv_a(r0, lanes)).astype(BF16)
            cv_ref[rows, lanes] = conv_b(r0, lanes)
            u = ext_ref[2, HALO + r0:HALO + r0 + MIX_RC, lanes]
            tot = u
            for back in range(1, win):
                tot = tot + ext_ref[2, HALO + r0 - back:HALO + r0 - back + MIX_RC, lanes]
            pos = i * tm + r0 + row
            cnt = jnp.minimum(pos + 1, win).astype(F32)
            pooled = tot / cnt - u
            yc = jnp.dot(pooled.astype(BF16), pm_ref[g], preferred_element_type=F32)
            yc_ref[rows, lanes] = (yc * ps_ref[:, lanes]).astype(BF16)

    cv = cv_ref[...]
    mu = jnp.mean(cv, axis=-1, keepdims=True)
    xc = cv - mu
    ln = xc * lax.rsqrt(jnp.mean(xc * xc, axis=-1, keepdims=True) + EPS)
    ln = ln * lng_ref[...] + lnb_ref[...]
    yb_ref[...] = (ln * jax.nn.sigmoid(ln)).astype(BF16)

    def head_rms(x, gain):
        sq = x * x
        hi = sq.astype(BF16)
        lo = (sq - hi.astype(F32)).astype(BF16)
        ss = (jnp.dot(hi, bd_ref[...], preferred_element_type=F32)
              + jnp.dot(lo, bd_ref[...], preferred_element_type=F32))
        return x * lax.rsqrt(ss * (1.0 / HEAD_DIM) + EPS) * gain

    q_ref[...] = (head_rms(slot(6), qg_ref[...]) * (LOG2E * HEAD_DIM ** -0.5)).astype(BF16)
    k_ref[...] = head_rms(slot(7), kg_ref[...]).astype(BF16)
    v_ref[...] = slot(8).astype(BF16)


def _mix(x, g, w_in, conv_a, conv_b, ln_g, ln_b, pool_map, pool_scale, q_gain, k_gain, blockdiag, l,
         cast_ws):
    s, d = x.shape
    w = WIDTH
    tm = MIX_TM
    cast_in, cast_out, cast_shapes = _cast_specs(cast_ws, l, s // tm)
    layer3 = lambda i: (l, 0, 0)
    row_blk = lambda i: (i, 0)
    out_w = jax.ShapeDtypeStruct((s, w), BF16)
    return pl.pallas_call(
        _mix_kernel,
        grid=(s // tm,),
        in_specs=[
            pl.BlockSpec((tm, d), row_blk),
            pl.BlockSpec((None, 1, d), layer3),
            pl.BlockSpec((None, d, N_SLOTS * w), layer3, pipeline_mode=pl.Buffered(1)),
            pl.BlockSpec((None, SC_TAPS, w), layer3),
            pl.BlockSpec((None, CONF_TAPS, w), layer3),
            pl.BlockSpec((None, 1, w), layer3),
            pl.BlockSpec((None, 1, w), layer3),
            pl.BlockSpec((None, len(POOL_WINDOWS), POOL_GROUP, POOL_GROUP), lambda i: (l, 0, 0, 0)),
            pl.BlockSpec((None, 1, w), layer3),
            pl.BlockSpec((None, 1, w), layer3),
            pl.BlockSpec((None, 1, w), layer3),
            pl.BlockSpec((w, w), lambda i: (0, 0)),
        ] + cast_in,
        out_specs=[pl.BlockSpec((tm, d), row_blk)] + [pl.BlockSpec((tm, w), row_blk)] * 6 + cast_out,
        out_shape=[jax.ShapeDtypeStruct((s, d), BF16)] + [out_w] * 6 + cast_shapes,
        scratch_shapes=[
            pltpu.VMEM((tm, N_SLOTS * w), F32),
            pltpu.VMEM((3, HALO + tm, w), F32),
            pltpu.VMEM((SUBLANES - 1, HALO + tm - SUBLANES, w), F32),
            pltpu.VMEM((tm, w), F32),
        ],
        compiler_params=pltpu.CompilerParams(
            dimension_semantics=("arbitrary",), vmem_limit_bytes=VMEM_LIMIT),
        name="mix",
    )(x, g, w_in, conv_a, conv_b, ln_g, ln_b, pool_map, pool_scale, q_gain, k_gain, blockdiag,
      *cast_ws)


def _attn_kernel(*refs, n_cast):
    q_ref, k_ref, v_ref, u_ref = refs[:4]
    o_ref = refs[4 + n_cast]
    qm_ref, acc_ref, carry_ref, gain_ref, cost_ref, prob_ref = refs[5 + 2 * n_cast:]
    _cast_slabs(refs[4:4 + n_cast], refs[5 + n_cast:5 + 2 * n_cast])
    qi = pl.program_id(0)
    tq, tk = ATT_TQ, ATT_TK
    n_tiles = WIDTH // LANES
    hpt = HEADS_PER_TILE

    lane = lax.broadcasted_iota(jnp.int32, (tq, LANES), 1)
    for t in range(n_tiles):
        qt = q_ref[:, t * LANES:(t + 1) * LANES]
        for hh in range(hpt):
            mine = (lane >= hh * HEAD_DIM) & (lane < (hh + 1) * HEAD_DIM)
            qm_ref[t, hh * tq:(hh + 1) * tq, :] = jnp.where(mine, qt, jnp.zeros_like(qt))
    acc_ref[...] = jnp.zeros_like(acc_ref)
    carry_ref[...] = jnp.zeros_like(carry_ref)

    def step(kjs, diagonal):
        k0s = [pl.multiple_of(kj * tk, tk) for kj in kjs]
        if diagonal:
            r = lax.broadcasted_iota(jnp.int32, (tq, tk), 0) + qi * tq
            c = lax.broadcasted_iota(jnp.int32, (tq, tk), 1) + k0s[0]
            mask = jnp.concatenate([c < r] * hpt, axis=0)
        units = [(kk, t) for kk in range(len(kjs)) for t in range(n_tiles)]

        def scores(u):
            kk, t = units[u]
            return lax.dot_general(qm_ref[t], k_ref[pl.ds(k0s[kk], tk), t * LANES:(t + 1) * LANES],
                                   (((1,), (1,)), ((), ())), preferred_element_type=F32)

        def costs(u, z):
            lg = jnp.log2(1.0 + jnp.exp2(-jnp.abs(z)))
            cost = jnp.maximum(z, 0.0) + lg
            gain_ref[u] = z - cost
            if diagonal:
                cost = jnp.where(mask, cost, 0.0)
            cost_ref[u] = cost.astype(BF16)
            tot = jnp.sum(cost, axis=1, keepdims=True)
            return tot, jnp.dot(cost_ref[u], u_ref[...], preferred_element_type=F32)

        def weights(u, tot, within):
            kk, t = units[u]
            carry = carry_ref[t]
            a = jnp.exp2(gain_ref[u] - (within + carry))
            if diagonal:
                a = jnp.where(mask, a, 0.0)
            carry_ref[t] = carry + tot
            prob_ref[u] = a.astype(BF16)
            pv = jnp.dot(prob_ref[u], v_ref[pl.ds(k0s[kk], tk), t * LANES:(t + 1) * LANES],
                         preferred_element_type=F32)
            out = pv[(hpt - 1) * tq:hpt * tq]
            for hh in range(hpt - 2, -1, -1):
                out = jnp.where(lane < (hh + 1) * HEAD_DIM, pv[hh * tq:(hh + 1) * tq], out)
            acc_ref[t] += out

        zs, mids = {}, {}
        for s in range(len(units) + 2 * ATT_SKEW):
            if s < len(units):
                zs[s] = scores(s)
            if 0 <= s - ATT_SKEW < len(units):
                mids[s - ATT_SKEW] = costs(s - ATT_SKEW, zs.pop(s - ATT_SKEW))
            if 0 <= s - 2 * ATT_SKEW < len(units):
                weights(s - 2 * ATT_SKEW, *mids.pop(s - 2 * ATT_SKEW))

    diag = ((qi + 1) * tq - 1) // tk
    step([diag], True)
    n_single = diag % ATT_UNROLL

    def single(n, c):
        step([diag - 1 - n], False)
        return c
    lax.fori_loop(0, n_single, single, 0)

    def group(n, c):
        top = diag - 1 - n_single - n * ATT_UNROLL
        step([top - kk for kk in range(ATT_UNROLL)], False)
        return c
    lax.fori_loop(0, diag // ATT_UNROLL, group, 0)

    for t in range(n_tiles):
        o_ref[:, t * LANES:(t + 1) * LANES] = acc_ref[t].astype(BF16)


def _attn(q, k, v, tri, cast_ws=(), l_cast=0):
    s, w = q.shape
    tq, tk = ATT_TQ, ATT_TK
    nt, hpt = w // LANES, HEADS_PER_TILE
    cast_in, cast_out, cast_shapes = _cast_specs(cast_ws, l_cast, s // tq)
    resident = dict(pipeline_mode=pl.Buffered(1))
    return pl.pallas_call(
        functools.partial(_attn_kernel, n_cast=len(cast_ws)),
        grid=(s // tq,),
        in_specs=[
            pl.BlockSpec((tq, w), lambda i: (i, 0)),
            pl.BlockSpec((s, w), lambda i: (0, 0), **resident),
            pl.BlockSpec((s, w), lambda i: (0, 0), **resident),
            pl.BlockSpec((tk, tk), lambda i: (0, 0), **resident),
        ] + cast_in,
        out_specs=[pl.BlockSpec((tq, w), lambda i: (i, 0))] + cast_out,
        out_shape=[jax.ShapeDtypeStruct((s, w), BF16)] + cast_shapes,
        scratch_shapes=[
            pltpu.VMEM((nt, hpt * tq, LANES), BF16),
            pltpu.VMEM((nt, tq, LANES), F32),
            pltpu.VMEM((nt, hpt * tq, 1), F32),
            pltpu.VMEM((ATT_UNROLL * nt, hpt * tq, tk), F32),
            pltpu.VMEM((ATT_UNROLL * nt, hpt * tq, tk), BF16),
            pltpu.VMEM((ATT_UNROLL * nt, hpt * tq, tk), BF16),
        ],
        compiler_params=pltpu.CompilerParams(
            dimension_semantics=("arbitrary",), vmem_limit_bytes=VMEM_LIMIT),
        name="attn",
    )(q, k, v, tri, *cast_ws)


def _gate_kernel(h_ref, ya_ref, yb_ref, yc_ref, yd_ref, wg_ref, wb_ref, o_ref, acc_ref):
    b = pl.program_id(2)

    @pl.when((pl.program_id(0) == 0) & (pl.program_id(1) == 0) & (b == 0))
    def _():
        acc_ref[...] = jnp.zeros_like(acc_ref)

    wg = wg_ref[...].astype(BF16)
    wb = wb_ref[...].astype(BF16)
    for c in range(h_ref.shape[0] // GATE_RC):
        rows = pl.ds(c * GATE_RC, GATE_RC)
        gate = jax.nn.sigmoid(jnp.dot(h_ref[rows, :], wg, preferred_element_type=F32))
        y = yd_ref[rows, :]
        for n, y_ref in reversed(list(enumerate((ya_ref, yb_ref, yc_ref)))):
            y = jnp.where(b == n, y_ref[rows, :], y)
        term = gate * jnp.dot(y, wb, preferred_element_type=F32)
        acc = jnp.where(b == 0, term, acc_ref[rows, :] + term)
        acc_ref[rows, :] = acc
        o_ref[rows, :] = acc.astype(BF16)


def _gate(h, ys, w_gate, w_branch, l):
    s, d = h.shape
    w = WIDTH
    tm, tn = GATE_TM, GATE_TN
    nn = d // tn
    y_spec = pl.BlockSpec((tm, w), lambda i, n, b: (i, 0))
    return pl.pallas_call(
        _gate_kernel,
        grid=(s // tm, nn, N_BRANCH),
        in_specs=[
            pl.BlockSpec((tm, d), lambda i, n, b: (i, 0)),
            y_spec, y_spec, y_spec, y_spec,
            pl.BlockSpec((None, d, tn), lambda i, n, b: (l, 0, b * nn + n)),
            pl.BlockSpec((None, None, w, tn), lambda i, n, b: (l, b, 0, n)),
        ],
        out_specs=pl.BlockSpec((tm, tn), lambda i, n, b: (i, n)),
        out_shape=jax.ShapeDtypeStruct((s, d), BF16),
        scratch_shapes=[pltpu.VMEM((tm, tn), F32)],
        compiler_params=pltpu.CompilerParams(
            dimension_semantics=("arbitrary", "arbitrary", "arbitrary"),
            vmem_limit_bytes=VMEM_LIMIT),
        name="gate",
    )(h, *ys, w_gate, w_branch)


def _proj_kernel(x_ref, m_ref, w_ref, o_ref):
    o_ref[...] = x_ref[...] + jnp.dot(m_ref[...], w_ref[...].astype(BF16),
                                      preferred_element_type=F32)


def _proj(x, merged, w_out, l):
    s, d = x.shape
    tm = PROJ_TM
    return pl.pallas_call(
        _proj_kernel,
        grid=(s // tm,),
        in_specs=[
            pl.BlockSpec((tm, d), lambda i: (i, 0)),
            pl.BlockSpec((tm, d), lambda i: (i, 0)),
            pl.BlockSpec((None, d, d), lambda i: (l, 0, 0), pipeline_mode=pl.Buffered(1)),
        ],
        out_specs=pl.BlockSpec((tm, d), lambda i: (i, 0)),
        out_shape=jax.ShapeDtypeStruct((s, d), F32),
        compiler_params=pltpu.CompilerParams(
            dimension_semantics=("arbitrary",), vmem_limit_bytes=VMEM_LIMIT),
        name="proj",
    )(x, merged, w_out)


def _constants():
    j = lax.broadcasted_iota(jnp.int32, (ATT_TK, ATT_TK), 0)
    c = lax.broadcasted_iota(jnp.int32, (ATT_TK, ATT_TK), 1)
    tri = (j > c).astype(BF16)
    hr = lax.broadcasted_iota(jnp.int32, (WIDTH, WIDTH), 0) // HEAD_DIM
    hc = lax.broadcasted_iota(jnp.int32, (WIDTH, WIDTH), 1) // HEAD_DIM
    blockdiag = (hr == hc).astype(BF16)
    return tri, blockdiag


def kernel(x, ffn1_norm, ffn1_w1, ffn1_w3, ffn1_w2, mix_norm, w_in, conv_a, conv_b, ln_b_gain,
           ln_b_bias, pool_map, pool_scale, q_norm, k_norm, w_branch, w_gate, w_out, ffn2_norm,
           ffn2_w1, ffn2_w3, ffn2_w2):
    b, s, d = x.shape
    depth = w_in.shape[0]
    tri, blockdiag = _constants()
    bf = lambda a: a.astype(BF16)
    row = lambda a: a.reshape(depth, 1, -1)
    win, pmap = bf(w_in), bf(pool_map)
    f1g, f2g, mg = row(ffn1_norm), row(ffn2_norm), row(mix_norm)
    lng, lnb, psc = row(ln_b_gain), row(ln_b_bias), row(pool_scale)
    qg, kg = row(jnp.tile(q_norm, (1, HEADS))), row(jnp.tile(k_norm, (1, HEADS)))
    ffn1_ws, ffn2_ws = (ffn1_w1, ffn1_w3, ffn1_w2), (ffn2_w1, ffn2_w3, ffn2_w2)
    outs = []
    for bi in range(b):
        xb = x[bi]
        f1 = [bf(wt[0]) for wt in ffn1_ws]
        for l in range(depth):
            xb = _ffn(xb, f1g, l, *f1)
            h, ya, yb, yc, q, k, v, *f2 = _mix(xb, mg, win, conv_a, conv_b, lng, lnb, pmap, psc,
                                               qg, kg, blockdiag, l, ffn2_ws)
            if l + 1 < depth:
                yd, *f1 = _attn(q, k, v, tri, ffn1_ws, l + 1)
            else:
                yd, = _attn(q, k, v, tri)
            merged = _gate(h, (ya, yb, yc, yd), w_gate, w_branch, l)
            xb = _proj(xb, merged, w_out, l)
            xb = _ffn(xb, f2g, l, *f2)
        outs.append(xb)
    return jnp.stack(outs, axis=0)
```

```python
import functools

import jax
import jax.numpy as jnp
from jax import lax
from jax.experimental import pallas as pl
from jax.experimental.pallas import tpu as pltpu

F32 = jnp.float32
BF16 = jnp.bfloat16

EPS = 1e-6
LOG2E = 1.4426950408889634
N_BRANCH = 4
WIDTH = 512
N_SLOTS = 9
SC_TAPS = 3
CONF_TAPS = 31
POOL_WINDOWS = (2, 4, 8, 16)
POOL_GROUP = WIDTH // len(POOL_WINDOWS)
HEADS = 8
HEAD_DIM = WIDTH // HEADS
LANES = 128
SUBLANES = 8
HEADS_PER_TILE = LANES // HEAD_DIM

FFN_TM = 1024
FFN_TF = 512
FFN_RC = 256
MIX_TM = 256
HALO = 32
MIX_RC = 64
ATT_TQ = 256
ATT_TK = 256
ATT_SKEW = 3
ATT_UNROLL = 2
GATE_TM = 1024
GATE_TN = 1024
GATE_RC = 256
PROJ_TM = 512
VMEM_LIMIT = 56 * 1024 * 1024


def _rms(x, g):
    return x * lax.rsqrt(jnp.mean(x * x, axis=-1, keepdims=True) + EPS) * g


def _ffn_kernel(x_ref, g_ref, w1_ref, w3_ref, w2_ref, o_ref, h_ref):
    j = pl.program_id(1)
    n_chunks = x_ref.shape[0] // FFN_RC

    def step(first):
        def up(c):
            rows = pl.ds(c * FFN_RC, FFN_RC)
            if first:
                h_ref[rows, :] = _rms(x_ref[rows, :], g_ref[...]).astype(BF16)
            h = h_ref[rows, :]
            return (jnp.dot(h, w1_ref[...], preferred_element_type=F32),
                    jnp.dot(h, w3_ref[...], preferred_element_type=F32))

        def down(c, a, b):
            rows = pl.ds(c * FFN_RC, FFN_RC)
            act = (a * jax.nn.sigmoid(a)) * b * 0.5
            base = x_ref[rows, :] if first else o_ref[rows, :]
            o_ref[rows, :] = base + jnp.dot(act.astype(BF16), w2_ref[...],
                                            preferred_element_type=F32)

        pending = up(0)
        for c in range(n_chunks):
            nxt = up(c + 1) if c + 1 < n_chunks else None
            down(c, *pending)
            pending = nxt

    pl.when(j == 0)(functools.partial(step, True))
    pl.when(j > 0)(functools.partial(step, False))


def _ffn(x, g, l, w1, w3, w2):
    s, d = x.shape
    f = w1.shape[1]
    return pl.pallas_call(
        _ffn_kernel,
        grid=(s // FFN_TM, f // FFN_TF),
        in_specs=[
            pl.BlockSpec((FFN_TM, d), lambda i, j: (i, 0)),
            pl.BlockSpec((None, 1, d), lambda i, j: (l, 0, 0)),
            pl.BlockSpec((d, FFN_TF), lambda i, j: (0, j)),
            pl.BlockSpec((d, FFN_TF), lambda i, j: (0, j)),
            pl.BlockSpec((FFN_TF, d), lambda i, j: (j, 0)),
        ],
        out_specs=pl.BlockSpec((FFN_TM, d), lambda i, j: (i, 0)),
        out_shape=jax.ShapeDtypeStruct((s, d), F32),
        scratch_shapes=[pltpu.VMEM((FFN_TM, d), BF16)],
        compiler_params=pltpu.CompilerParams(
            dimension_semantics=("arbitrary", "arbitrary"), vmem_limit_bytes=VMEM_LIMIT),
        name="ffn",
    )(x, g, w1, w3, w2)


def _cast_specs(ws, l, n_steps):
    in_specs, out_specs, out_shapes = [], [], []
    for w in ws:
        _, r, c = w.shape
        rows = r // n_steps
        assert rows * n_steps == r and rows % 16 == 0, (r, n_steps)
        in_specs.append(pl.BlockSpec((None, rows, c), lambda i: (l, i, 0)))
        out_specs.append(pl.BlockSpec((rows, c), lambda i: (i, 0)))
        out_shapes.append(jax.ShapeDtypeStruct((r, c), BF16))
    return in_specs, out_specs, out_shapes


def _cast_slabs(in_refs, out_refs):
    for src, dst in zip(in_refs, out_refs):
        dst[...] = src[...].astype(BF16)


def _mix_kernel(x_ref, g_ref, win_ref, ca_ref, cb_ref, lng_ref, lnb_ref, pm_ref, ps_ref,
                qg_ref, kg_ref, bd_ref, wa_ref, wb_ref, wc_ref,
                h_ref, ya_ref, yb_ref, yc_ref, q_ref, k_ref, v_ref, wa_out, wb_out, wc_out,
                p_ref, ext_ref, sh_ref, cv_ref):
    i = pl.program_id(0)
    tm = x_ref.shape[0]
    w = WIDTH
    _cast_slabs((wa_ref, wb_ref, wc_ref), (wa_out, wb_out, wc_out))

    @pl.when(i == 0)
    def _():
        ext_ref[:, 0:HALO, :] = jnp.zeros((3, HALO, w), F32)

    @pl.when(i > 0)
    def _():
        ext_ref[:, 0:HALO, :] = ext_ref[:, tm:tm + HALO, :]

    h_ref[...] = _rms(x_ref[...], g_ref[...]).astype(BF16)
    for s in range(N_SLOTS):
        cols = slice(s * w, (s + 1) * w)
        p_ref[:, cols] = jnp.dot(h_ref[...], win_ref[:, cols], preferred_element_type=F32)

    def slot(s, rows=slice(None), lanes=slice(0, w)):
        return p_ref[rows, s * w + lanes.start:s * w + lanes.stop]

    ext_ref[0, HALO:HALO + tm, :] = slot(2) * slot(0)
    ext_ref[1, HALO:HALO + tm, :] = slot(3) * jax.nn.sigmoid(slot(4))
    ext_ref[2, HALO:HALO + tm, :] = slot(5)

    n_sh = HALO + tm - SUBLANES
    for r in range(1, SUBLANES):
        sh_ref[r - 1] = ext_ref[1, r:r + n_sh, :]

    def conv_a(r0, lanes):
        acc = None
        for k in range(SC_TAPS):
            start = HALO + r0 - (SC_TAPS - 1) + k
            term = ca_ref[k:k + 1, lanes] * ext_ref[0, start:start + MIX_RC, lanes]
            acc = term if acc is None else acc + term
        return acc

    def conv_b(r0, lanes):
        acc = None
        for k in range(CONF_TAPS):
            start = HALO + r0 - (CONF_TAPS - 1) + k
            r = start % SUBLANES
            base = start - r
            if r == 0:
                u = ext_ref[1, base:base + MIX_RC, lanes]
            else:
                u = sh_ref[r - 1, base:base + MIX_RC, lanes]
            term = cb_ref[k:k + 1, lanes] * u
            acc = term if acc is None else acc + term
        return acc

    row = lax.broadcasted_iota(jnp.int32, (MIX_RC, LANES), 0)
    for r0 in range(0, tm, MIX_RC):
        rows = slice(r0, r0 + MIX_RC)
        for g, win in enumerate(POOL_WINDOWS):
            lanes = slice(g * LANES, (g + 1) * LANES)
            ya_ref[rows, lanes] = (slot(1, rows, lanes) * conv_a(r0, lanes)).astype(BF16)
            cv_ref[rows, lanes] = conv_b(r0, lanes)
            u = ext_ref[2, HALO + r0:HALO + r0 + MIX_RC, lanes]
            tot = u
            for back in range(1, win):
                tot = tot + ext_ref[2, HALO + r0 - back:HALO + r0 - back + MIX_RC, lanes]
            pos = i * tm + r0 + row
            cnt = jnp.minimum(pos + 1, win).astype(F32)
            pooled = tot / cnt - u
            yc = jnp.dot(pooled.astype(BF16), pm_ref[g], preferred_element_type=F32)
            yc_ref[rows, lanes] = (yc * ps_ref[:, lanes]).astype(BF16)

    cv = cv_ref[...]
    mu = jnp.mean(cv, axis=-1, keepdims=True)
    xc = cv - mu
    ln = xc * lax.rsqrt(jnp.mean(xc * xc, axis=-1, keepdims=True) + EPS)
    ln = ln * lng_ref[...] + lnb_ref[...]
    yb_ref[...] = (ln * jax.nn.sigmoid(ln)).astype(BF16)

    def head_rms(x, gain):
        sq = x * x
        hi = sq.astype(BF16)
        lo = (sq - hi.astype(F32)).astype(BF16)
        ss = (jnp.dot(hi, bd_ref[...], preferred_element_type=F32)
              + jnp.dot(lo, bd_ref[...], preferred_element_type=F32))
        return x * lax.rsqrt(ss * (1.0 / HEAD_DIM) + EPS) * gain

    q_ref[...] = (head_rms(slot(6), qg_ref[...]) * (LOG2E * HEAD_DIM ** -0.5)).astype(BF16)
    k_ref[...] = head_rms(slot(7), kg_ref[...]).astype(BF16)
    v_ref[...] = slot(8).astype(BF16)


def _mix(x, g, w_in, conv_a, conv_b, ln_g, ln_b, pool_map, pool_scale, q_gain, k_gain, blockdiag, l,
         cast_ws):
    s, d = x.shape
    w = WIDTH
    tm = MIX_TM
    cast_in, cast_out, cast_shapes = _cast_specs(cast_ws, l, s // tm)
    layer3 = lambda i: (l, 0, 0)
    row_blk = lambda i: (i, 0)
    out_w = jax.ShapeDtypeStruct((s, w), BF16)
    return pl.pallas_call(
        _mix_kernel,
        grid=(s // tm,),
        in_specs=[
            pl.BlockSpec((tm, d), row_blk),
            pl.BlockSpec((None, 1, d), layer3),
            pl.BlockSpec((d, N_SLOTS * w), lambda i: (0, 0), pipeline_mode=pl.Buffered(1)),
            pl.BlockSpec((None, SC_TAPS, w), layer3),
            pl.BlockSpec((None, CONF_TAPS, w), layer3),
            pl.BlockSpec((None, 1, w), layer3),
            pl.BlockSpec((None, 1, w), layer3),
            pl.BlockSpec((None, len(POOL_WINDOWS), POOL_GROUP, POOL_GROUP), lambda i: (l, 0, 0, 0)),
            pl.BlockSpec((None, 1, w), layer3),
            pl.BlockSpec((None, 1, w), layer3),
            pl.BlockSpec((None, 1, w), layer3),
            pl.BlockSpec((w, w), lambda i: (0, 0)),
        ] + cast_in,
        out_specs=[pl.BlockSpec((tm, d), row_blk)] + [pl.BlockSpec((tm, w), row_blk)] * 6 + cast_out,
        out_shape=[jax.ShapeDtypeStruct((s, d), BF16)] + [out_w] * 6 + cast_shapes,
        scratch_shapes=[
            pltpu.VMEM((tm, N_SLOTS * w), F32),
            pltpu.VMEM((3, HALO + tm, w), F32),
            pltpu.VMEM((SUBLANES - 1, HALO + tm - SUBLANES, w), F32),
            pltpu.VMEM((tm, w), F32),
        ],
        compiler_params=pltpu.CompilerParams(
            dimension_semantics=("arbitrary",), vmem_limit_bytes=VMEM_LIMIT),
        name="mix",
    )(x, g, w_in, conv_a, conv_b, ln_g, ln_b, pool_map, pool_scale, q_gain, k_gain, blockdiag,
      *cast_ws)


def _attn_kernel(*refs, n_cast):
    q_ref, k_ref, v_ref, u_ref = refs[:4]
    o_ref = refs[4 + n_cast]
    qm_ref, acc_ref, carry_ref, gain_ref, cost_ref, prob_ref = refs[5 + 2 * n_cast:]
    _cast_slabs(refs[4:4 + n_cast], refs[5 + n_cast:5 + 2 * n_cast])
    qi = pl.program_id(0)
    tq, tk = ATT_TQ, ATT_TK
    n_tiles = WIDTH // LANES
    hpt = HEADS_PER_TILE

    lane = lax.broadcasted_iota(jnp.int32, (tq, LANES), 1)
    for t in range(n_tiles):
        qt = q_ref[:, t * LANES:(t + 1) * LANES]
        for hh in range(hpt):
            mine = (lane >= hh * HEAD_DIM) & (lane < (hh + 1) * HEAD_DIM)
            qm_ref[t, hh * tq:(hh + 1) * tq, :] = jnp.where(mine, qt, jnp.zeros_like(qt))
    acc_ref[...] = jnp.zeros_like(acc_ref)
    carry_ref[...] = jnp.zeros_like(carry_ref)

    def step(kjs, diagonal):
        k0s = [pl.multiple_of(kj * tk, tk) for kj in kjs]
        if diagonal:
            r = lax.broadcasted_iota(jnp.int32, (tq, tk), 0) + qi * tq
            c = lax.broadcasted_iota(jnp.int32, (tq, tk), 1) + k0s[0]
            mask = jnp.concatenate([c < r] * hpt, axis=0)
        units = [(kk, t) for kk in range(len(kjs)) for t in range(n_tiles)]

        def scores(u):
            kk, t = units[u]
            return lax.dot_general(qm_ref[t], k_ref[pl.ds(k0s[kk], tk), t * LANES:(t + 1) * LANES],
                                   (((1,), (1,)), ((), ())), preferred_element_type=F32)

        def costs(u, z):
            lg = jnp.log2(1.0 + jnp.exp2(-jnp.abs(z)))
            cost = jnp.maximum(z, 0.0) + lg
            gain_ref[u] = z - cost
            if diagonal:
                cost = jnp.where(mask, cost, 0.0)
            cost_ref[u] = cost.astype(BF16)
            tot = jnp.sum(cost, axis=1, keepdims=True)
            return tot, jnp.dot(cost_ref[u], u_ref[...], preferred_element_type=F32)

        def weights(u, tot, within):
            kk, t = units[u]
            carry = carry_ref[t]
            a = jnp.exp2(gain_ref[u] - (within + carry))
            if diagonal:
                a = jnp.where(mask, a, 0.0)
            carry_ref[t] = carry + tot
            prob_ref[u] = a.astype(BF16)
            pv = jnp.dot(prob_ref[u], v_ref[pl.ds(k0s[kk], tk), t * LANES:(t + 1) * LANES],
                         preferred_element_type=F32)
            out = pv[(hpt - 1) * tq:hpt * tq]
            for hh in range(hpt - 2, -1, -1):
                out = jnp.where(lane < (hh + 1) * HEAD_DIM, pv[hh * tq:(hh + 1) * tq], out)
            acc_ref[t] += out

        zs, mids = {}, {}
        for s in range(len(units) + 2 * ATT_SKEW):
            if s < len(units):
                zs[s] = scores(s)
            if 0 <= s - ATT_SKEW < len(units):
                mids[s - ATT_SKEW] = costs(s - ATT_SKEW, zs.pop(s - ATT_SKEW))
            if 0 <= s - 2 * ATT_SKEW < len(units):
                weights(s - 2 * ATT_SKEW, *mids.pop(s - 2 * ATT_SKEW))

    diag = ((qi + 1) * tq - 1) // tk
    step([diag], True)
    n_single = diag % ATT_UNROLL

    def single(n, c):
        step([diag - 1 - n], False)
        return c
    lax.fori_loop(0, n_single, single, 0)

    def group(n, c):
        top = diag - 1 - n_single - n * ATT_UNROLL
        step([top - kk for kk in range(ATT_UNROLL)], False)
        return c
    lax.fori_loop(0, diag // ATT_UNROLL, group, 0)

    for t in range(n_tiles):
        o_ref[:, t * LANES:(t + 1) * LANES] = acc_ref[t].astype(BF16)


def _attn(q, k, v, tri, cast_ws=(), l_cast=0):
    s, w = q.shape
    tq, tk = ATT_TQ, ATT_TK
    nt, hpt = w // LANES, HEADS_PER_TILE
    cast_in, cast_out, cast_shapes = _cast_specs(cast_ws, l_cast, s // tq)
    resident = dict(pipeline_mode=pl.Buffered(1))
    return pl.pallas_call(
        functools.partial(_attn_kernel, n_cast=len(cast_ws)),
        grid=(s // tq,),
        in_specs=[
            pl.BlockSpec((tq, w), lambda i: (i, 0)),
            pl.BlockSpec((s, w), lambda i: (0, 0), **resident),
            pl.BlockSpec((s, w), lambda i: (0, 0), **resident),
            pl.BlockSpec((tk, tk), lambda i: (0, 0), **resident),
        ] + cast_in,
        out_specs=[pl.BlockSpec((tq, w), lambda i: (i, 0))] + cast_out,
        out_shape=[jax.ShapeDtypeStruct((s, w), BF16)] + cast_shapes,
        scratch_shapes=[
            pltpu.VMEM((nt, hpt * tq, LANES), BF16),
            pltpu.VMEM((nt, tq, LANES), F32),
            pltpu.VMEM((nt, hpt * tq, 1), F32),
            pltpu.VMEM((ATT_UNROLL * nt, hpt * tq, tk), F32),
            pltpu.VMEM((ATT_UNROLL * nt, hpt * tq, tk), BF16),
            pltpu.VMEM((ATT_UNROLL * nt, hpt * tq, tk), BF16),
        ],
        compiler_params=pltpu.CompilerParams(
            dimension_semantics=("arbitrary",), vmem_limit_bytes=VMEM_LIMIT),
        name="attn",
    )(q, k, v, tri, *cast_ws)


def _gate_kernel(h_ref, ya_ref, yb_ref, yc_ref, yd_ref, wg_ref, wb_ref, o_ref, acc_ref):
    b = pl.program_id(2)

    @pl.when((pl.program_id(0) == 0) & (pl.program_id(1) == 0) & (b == 0))
    def _():
        acc_ref[...] = jnp.zeros_like(acc_ref)

    wg = wg_ref[...].astype(BF16)
    wb = wb_ref[...].astype(BF16)
    for c in range(h_ref.shape[0] // GATE_RC):
        rows = pl.ds(c * GATE_RC, GATE_RC)
        gate = jax.nn.sigmoid(jnp.dot(h_ref[rows, :], wg, preferred_element_type=F32))
        y = yd_ref[rows, :]
        for n, y_ref in reversed(list(enumerate((ya_ref, yb_ref, yc_ref)))):
            y = jnp.where(b == n, y_ref[rows, :], y)
        term = gate * jnp.dot(y, wb, preferred_element_type=F32)
        acc = jnp.where(b == 0, term, acc_ref[rows, :] + term)
        acc_ref[rows, :] = acc
        o_ref[rows, :] = acc.astype(BF16)


def _gate(h, ys, w_gate, w_branch, l):
    s, d = h.shape
    w = WIDTH
    tm, tn = GATE_TM, GATE_TN
    nn = d // tn
    y_spec = pl.BlockSpec((tm, w), lambda i, n, b: (i, 0))
    return pl.pallas_call(
        _gate_kernel,
        grid=(s // tm, nn, N_BRANCH),
        in_specs=[
            pl.BlockSpec((tm, d), lambda i, n, b: (i, 0)),
            y_spec, y_spec, y_spec, y_spec,
            pl.BlockSpec((None, d, tn), lambda i, n, b: (l, 0, b * nn + n)),
            pl.BlockSpec((None, None, w, tn), lambda i, n, b: (l, b, 0, n)),
        ],
        out_specs=pl.BlockSpec((tm, tn), lambda i, n, b: (i, n)),
        out_shape=jax.ShapeDtypeStruct((s, d), BF16),
        scratch_shapes=[pltpu.VMEM((tm, tn), F32)],
        compiler_params=pltpu.CompilerParams(
            dimension_semantics=("arbitrary", "arbitrary", "arbitrary"),
            vmem_limit_bytes=VMEM_LIMIT),
        name="gate",
    )(h, *ys, w_gate, w_branch)


def _proj_kernel(x_ref, m_ref, w_ref, o_ref):
    o_ref[...] = x_ref[...] + jnp.dot(m_ref[...], w_ref[...].astype(BF16),
                                      preferred_element_type=F32)


def _proj(x, merged, w_out, l):
    s, d = x.shape
    tm = PROJ_TM
    return pl.pallas_call(
        _proj_kernel,
        grid=(s // tm,),
        in_specs=[
            pl.BlockSpec((tm, d), lambda i: (i, 0)),
            pl.BlockSpec((tm, d), lambda i: (i, 0)),
            pl.BlockSpec((None, d, d), lambda i: (l, 0, 0), pipeline_mode=pl.Buffered(1)),
        ],
        out_specs=pl.BlockSpec((tm, d), lambda i: (i, 0)),
        out_shape=jax.ShapeDtypeStruct((s, d), F32),
        compiler_params=pltpu.CompilerParams(
            dimension_semantics=("arbitrary",), vmem_limit_bytes=VMEM_LIMIT),
        name="proj",
    )(x, merged, w_out)


def _constants():
    j = lax.broadcasted_iota(jnp.int32, (ATT_TK, ATT_TK), 0)
    c = lax.broadcasted_iota(jnp.int32, (ATT_TK, ATT_TK), 1)
    tri = (j > c).astype(BF16)
    hr = lax.broadcasted_iota(jnp.int32, (WIDTH, WIDTH), 0) // HEAD_DIM
    hc = lax.broadcasted_iota(jnp.int32, (WIDTH, WIDTH), 1) // HEAD_DIM
    blockdiag = (hr == hc).astype(BF16)
    return tri, blockdiag


def kernel(x, ffn1_norm, ffn1_w1, ffn1_w3, ffn1_w2, mix_norm, w_in, conv_a, conv_b, ln_b_gain,
           ln_b_bias, pool_map, pool_scale, q_norm, k_norm, w_branch, w_gate, w_out, ffn2_norm,
           ffn2_w1, ffn2_w3, ffn2_w2):
    b, s, d = x.shape
    depth = w_in.shape[0]
    tri, blockdiag = _constants()
    bf = lambda a: a.astype(BF16)
    row = lambda a: a.reshape(depth, 1, -1)
    pmap = bf(pool_map)
    f1g, f2g, mg = row(ffn1_norm), row(ffn2_norm), row(mix_norm)
    lng, lnb, psc = row(ln_b_gain), row(ln_b_bias), row(pool_scale)
    qg, kg = row(jnp.tile(q_norm, (1, HEADS))), row(jnp.tile(k_norm, (1, HEADS)))
    ffn2_ws = (ffn2_w1, ffn2_w3, ffn2_w2)
    next_ws = (ffn1_w1, ffn1_w3, ffn1_w2, w_in)
    outs = []
    for bi in range(b):
        xb = x[bi]
        *f1, win = [bf(wt[0]) for wt in next_ws]
        for l in range(depth):
            xb = _ffn(xb, f1g, l, *f1)
            h, ya, yb, yc, q, k, v, *f2 = _mix(xb, mg, win, conv_a, conv_b, lng, lnb, pmap, psc,
                                               qg, kg, blockdiag, l, ffn2_ws)
            if l + 1 < depth:
                yd, *f1, win = _attn(q, k, v, tri, next_ws, l + 1)
            else:
                yd, = _attn(q, k, v, tri)
            merged = _gate(h, (ya, yb, yc, yd), w_gate, w_branch, l)
            xb = _proj(xb, merged, w_out, l)
            xb = _ffn(xb, f2g, l, *f2)
        outs.append(xb)
    return jnp.stack(outs, axis=0)
```

```python
import functools

import jax
import jax.numpy as jnp
from jax import lax
from jax.experimental import pallas as pl
from jax.experimental.pallas import tpu as pltpu

F32 = jnp.float32
BF16 = jnp.bfloat16

EPS = 1e-6
LOG2E = 1.4426950408889634
N_BRANCH = 4
WIDTH = 512
N_SLOTS = 9
SC_TAPS = 3
CONF_TAPS = 31
POOL_WINDOWS = (2, 4, 8, 16)
POOL_GROUP = WIDTH // len(POOL_WINDOWS)
HEADS = 8
HEAD_DIM = WIDTH // HEADS
LANES = 128
SUBLANES = 8
HEADS_PER_TILE = LANES // HEAD_DIM

FFN_TM = 1024
FFN_TF = 512
FFN_RC = 256
MIX_TM = 256
HALO = 32
MIX_RC = 64
ATT_TQ = 256
ATT_TK = 256
ATT_SKEW = 3
ATT_UNROLL = 2
GATE_TM = 1024
GATE_TN = 1024
GATE_RC = 256
PROJ_TM = 512
VMEM_LIMIT = 56 * 1024 * 1024


def _rms(x, g):
    return x * lax.rsqrt(jnp.mean(x * x, axis=-1, keepdims=True) + EPS) * g


def _ffn_kernel(x_ref, g_ref, w1_ref, w3_ref, w2_ref, o_ref, h_ref):
    j = pl.program_id(1)
    n_chunks = x_ref.shape[0] // FFN_RC

    def step(first):
        def up(c):
            rows = pl.ds(c * FFN_RC, FFN_RC)
            if first:
                h_ref[rows, :] = _rms(x_ref[rows, :], g_ref[...]).astype(BF16)
            h = h_ref[rows, :]
            return (jnp.dot(h, w1_ref[...], preferred_element_type=F32),
                    jnp.dot(h, w3_ref[...], preferred_element_type=F32))

        def down(c, a, b):
            rows = pl.ds(c * FFN_RC, FFN_RC)
            act = (a * jax.nn.sigmoid(a)) * b * 0.5
            base = x_ref[rows, :] if first else o_ref[rows, :]
            o_ref[rows, :] = base + jnp.dot(act.astype(BF16), w2_ref[...],
                                            preferred_element_type=F32)

        pending = up(0)
        for c in range(n_chunks):
            nxt = up(c + 1) if c + 1 < n_chunks else None
            down(c, *pending)
            pending = nxt

    pl.when(j == 0)(functools.partial(step, True))
    pl.when(j > 0)(functools.partial(step, False))


def _ffn(x, g, l, w1, w3, w2):
    s, d = x.shape
    f = w1.shape[1]
    return pl.pallas_call(
        _ffn_kernel,
        grid=(s // FFN_TM, f // FFN_TF),
        in_specs=[
            pl.BlockSpec((FFN_TM, d), lambda i, j: (i, 0)),
            pl.BlockSpec((None, 1, d), lambda i, j: (l, 0, 0)),
            pl.BlockSpec((d, FFN_TF), lambda i, j: (0, j)),
            pl.BlockSpec((d, FFN_TF), lambda i, j: (0, j)),
            pl.BlockSpec((FFN_TF, d), lambda i, j: (j, 0)),
        ],
        out_specs=pl.BlockSpec((FFN_TM, d), lambda i, j: (i, 0)),
        out_shape=jax.ShapeDtypeStruct((s, d), F32),
        scratch_shapes=[pltpu.VMEM((FFN_TM, d), BF16)],
        compiler_params=pltpu.CompilerParams(
            dimension_semantics=("arbitrary", "arbitrary"), vmem_limit_bytes=VMEM_LIMIT),
        name="ffn",
    )(x, g, w1, w3, w2)


def _cast_specs(ws, l, n_steps):
    in_specs, out_specs, out_shapes = [], [], []
    for w in ws:
        _, r, c = w.shape
        rows = r // n_steps
        assert rows * n_steps == r and rows % 16 == 0, (r, n_steps)
        in_specs.append(pl.BlockSpec((None, rows, c), lambda i: (l, i, 0)))
        out_specs.append(pl.BlockSpec((rows, c), lambda i: (i, 0)))
        out_shapes.append(jax.ShapeDtypeStruct((r, c), BF16))
    return in_specs, out_specs, out_shapes


def _cast_slabs(in_refs, out_refs):
    for src, dst in zip(in_refs, out_refs):
        dst[...] = src[...].astype(BF16)


def _mix_kernel(x_ref, g_ref, win_ref, ca_ref, cb_ref, lng_ref, lnb_ref, pm_ref, ps_ref,
                qg_ref, kg_ref, bd_ref, wa_ref, wb_ref, wc_ref,
                h_ref, ya_ref, yb_ref, yc_ref, q_ref, k_ref, v_ref, wa_out, wb_out, wc_out,
                p_ref, ext_ref, sh_ref, cv_ref):
    i = pl.program_id(0)
    tm = x_ref.shape[0]
    w = WIDTH
    _cast_slabs((wa_ref, wb_ref, wc_ref), (wa_out, wb_out, wc_out))

    @pl.when(i == 0)
    def _():
        ext_ref[:, 0:HALO, :] = jnp.zeros((3, HALO, w), F32)

    @pl.when(i > 0)
    def _():
        ext_ref[:, 0:HALO, :] = ext_ref[:, tm:tm + HALO, :]

    h_ref[...] = _rms(x_ref[...], g_ref[...]).astype(BF16)
    for s in range(N_SLOTS):
        cols = slice(s * w, (s + 1) * w)
        p_ref[:, cols] = jnp.dot(h_ref[...], win_ref[:, cols], preferred_element_type=F32)

    def slot(s, rows=slice(None), lanes=slice(0, w)):
        return p_ref[rows, s * w + lanes.start:s * w + lanes.stop]

    ext_ref[0, HALO:HALO + tm, :] = slot(2) * slot(0)
    ext_ref[1, HALO:HALO + tm, :] = slot(3) * jax.nn.sigmoid(slot(4))
    ext_ref[2, HALO:HALO + tm, :] = slot(5)

    n_sh = HALO + tm - SUBLANES
    for r in range(1, SUBLANES):
        sh_ref[r - 1] = ext_ref[1, r:r + n_sh, :]

    def conv_a(r0, lanes):
        acc = None
        for k in range(SC_TAPS):
            start = HALO + r0 - (SC_TAPS - 1) + k
            term = ca_ref[k:k + 1, lanes] * ext_ref[0, start:start + MIX_RC, lanes]
            acc = term if acc is None else acc + term
        return acc

    def conv_b(r0, lanes):
        acc = None
        for k in range(CONF_TAPS):
            start = HALO + r0 - (CONF_TAPS - 1) + k
            r = start % SUBLANES
            base = start - r
            if r == 0:
                u = ext_ref[1, base:base + MIX_RC, lanes]
            else:
                u = sh_ref[r - 1, base:base + MIX_RC, lanes]
            term = cb_ref[k:k + 1, lanes] * u
            acc = term if acc is None else acc + term
        return acc

    row = lax.broadcasted_iota(jnp.int32, (MIX_RC, LANES), 0)
    for r0 in range(0, tm, MIX_RC):
        rows = slice(r0, r0 + MIX_RC)
        for g, win in enumerate(POOL_WINDOWS):
            lanes = slice(g * LANES, (g + 1) * LANES)
            ya_ref[rows, lanes] = (slot(1, rows, lanes) * conv_a(r0, lanes)).astype(BF16)
            cv_ref[rows, lanes] = conv_b(r0, lanes)
            u = ext_ref[2, HALO + r0:HALO + r0 + MIX_RC, lanes]
            tot = u
            for back in range(1, win):
                tot = tot + ext_ref[2, HALO + r0 - back:HALO + r0 - back + MIX_RC, lanes]
            pos = i * tm + r0 + row
            cnt = jnp.minimum(pos + 1, win).astype(F32)
            pooled = tot / cnt - u
            yc = jnp.dot(pooled.astype(BF16), pm_ref[g], preferred_element_type=F32)
            yc_ref[rows, lanes] = (yc * ps_ref[:, lanes]).astype(BF16)

    cv = cv_ref[...]
    mu = jnp.mean(cv, axis=-1, keepdims=True)
    xc = cv - mu
    ln = xc * lax.rsqrt(jnp.mean(xc * xc, axis=-1, keepdims=True) + EPS)
    ln = ln * lng_ref[...] + lnb_ref[...]
    yb_ref[...] = (ln * jax.nn.sigmoid(ln)).astype(BF16)

    def head_rms(x, gain):
        sq = x * x
        hi = sq.astype(BF16)
        lo = (sq - hi.astype(F32)).astype(BF16)
        ss = (jnp.dot(hi, bd_ref[...], preferred_element_type=F32)
              + jnp.dot(lo, bd_ref[...], preferred_element_type=F32))
        return x * lax.rsqrt(ss * (1.0 / HEAD_DIM) + EPS) * gain

    q_ref[...] = (head_rms(slot(6), qg_ref[...]) * (LOG2E * HEAD_DIM ** -0.5)).astype(BF16)
    k_ref[...] = head_rms(slot(7), kg_ref[...]).astype(BF16)
    v_ref[...] = slot(8).astype(BF16)


def _mix(x, g, w_in, conv_a, conv_b, ln_g, ln_b, pool_map, pool_scale, q_gain, k_gain, blockdiag, l,
         cast_ws):
    s, d = x.shape
    w = WIDTH
    tm = MIX_TM
    cast_in, cast_out, cast_shapes = _cast_specs(cast_ws, l, s // tm)
    layer3 = lambda i: (l, 0, 0)
    row_blk = lambda i: (i, 0)
    out_w = jax.ShapeDtypeStruct((s, w), BF16)
    return pl.pallas_call(
        _mix_kernel,
        grid=(s // tm,),
        in_specs=[
            pl.BlockSpec((tm, d), row_blk),
            pl.BlockSpec((None, 1, d), layer3),
            pl.BlockSpec((d, N_SLOTS * w), lambda i: (0, 0), pipeline_mode=pl.Buffered(1)),
            pl.BlockSpec((None, SC_TAPS, w), layer3),
            pl.BlockSpec((None, CONF_TAPS, w), layer3),
            pl.BlockSpec((None, 1, w), layer3),
            pl.BlockSpec((None, 1, w), layer3),
            pl.BlockSpec((None, len(POOL_WINDOWS), POOL_GROUP, POOL_GROUP), lambda i: (l, 0, 0, 0)),
            pl.BlockSpec((None, 1, w), layer3),
            pl.BlockSpec((None, 1, w), layer3),
            pl.BlockSpec((None, 1, w), layer3),
            pl.BlockSpec((w, w), lambda i: (0, 0)),
        ] + cast_in,
        out_specs=[pl.BlockSpec((tm, d), row_blk)] + [pl.BlockSpec((tm, w), row_blk)] * 6 + cast_out,
        out_shape=[jax.ShapeDtypeStruct((s, d), BF16)] + [out_w] * 6 + cast_shapes,
        scratch_shapes=[
            pltpu.VMEM((tm, N_SLOTS * w), F32),
            pltpu.VMEM((3, HALO + tm, w), F32),
            pltpu.VMEM((SUBLANES - 1, HALO + tm - SUBLANES, w), F32),
            pltpu.VMEM((tm, w), F32),
        ],
        compiler_params=pltpu.CompilerParams(
            dimension_semantics=("arbitrary",), vmem_limit_bytes=VMEM_LIMIT),
        name="mix",
    )(x, g, w_in, conv_a, conv_b, ln_g, ln_b, pool_map, pool_scale, q_gain, k_gain, blockdiag,
      *cast_ws)


def _attn_kernel(*refs, n_cast):
    q_ref, k_ref, v_ref, u_ref = refs[:4]
    o_ref = refs[4 + n_cast]
    qm_ref, acc_ref, carry_ref, gain_ref, cost_ref, prob_ref = refs[5 + 2 * n_cast:]
    _cast_slabs(refs[4:4 + n_cast], refs[5 + n_cast:5 + 2 * n_cast])
    qi = pl.program_id(0)
    tq, tk = ATT_TQ, ATT_TK
    n_tiles = WIDTH // LANES
    hpt = HEADS_PER_TILE

    lane = lax.broadcasted_iota(jnp.int32, (tq, LANES), 1)
    for t in range(n_tiles):
        qt = q_ref[:, t * LANES:(t + 1) * LANES]
        for hh in range(hpt):
            mine = (lane >= hh * HEAD_DIM) & (lane < (hh + 1) * HEAD_DIM)
            qm_ref[t, hh * tq:(hh + 1) * tq, :] = jnp.where(mine, qt, jnp.zeros_like(qt))
    acc_ref[...] = jnp.zeros_like(acc_ref)
    carry_ref[...] = jnp.zeros_like(carry_ref)

    def step(kjs, diagonal):
        k0s = [pl.multiple_of(kj * tk, tk) for kj in kjs]
        if diagonal:
            r = lax.broadcasted_iota(jnp.int32, (tq, tk), 0) + qi * tq
            c = lax.broadcasted_iota(jnp.int32, (tq, tk), 1) + k0s[0]
            mask = jnp.concatenate([c < r] * hpt, axis=0)
        units = [(kk, t) for kk in range(len(kjs)) for t in range(n_tiles)]

        def scores(u):
            kk, t = units[u]
            return lax.dot_general(qm_ref[t], k_ref[pl.ds(k0s[kk], tk), t * LANES:(t + 1) * LANES],
                                   (((1,), (1,)), ((), ())), preferred_element_type=F32)

        def costs(u, z):
            lg = jnp.log2(1.0 + jnp.exp2(-jnp.abs(z)))
            cost = jnp.maximum(z, 0.0) + lg
            gain_ref[u] = z - cost
            if diagonal:
                cost = jnp.where(mask, cost, 0.0)
            cost_ref[u] = cost.astype(BF16)
            sums = jnp.dot(cost_ref[u], u_ref[...], preferred_element_type=F32)
            return sums[:, tk:], sums[:, :tk]

        def weights(u, tot, within):
            kk, t = units[u]
            carry = carry_ref[t]
            a = jnp.exp2(gain_ref[u] - (within + jnp.concatenate([carry] * (tk // LANES), axis=1)))
            if diagonal:
                a = jnp.where(mask, a, 0.0)
            carry_ref[t] = carry + tot
            prob_ref[u] = a.astype(BF16)
            pv = jnp.dot(prob_ref[u], v_ref[pl.ds(k0s[kk], tk), t * LANES:(t + 1) * LANES],
                         preferred_element_type=F32)
            out = pv[(hpt - 1) * tq:hpt * tq]
            for hh in range(hpt - 2, -1, -1):
                out = jnp.where(lane < (hh + 1) * HEAD_DIM, pv[hh * tq:(hh + 1) * tq], out)
            acc_ref[t] += out

        zs, mids = {}, {}
        for s in range(len(units) + 2 * ATT_SKEW):
            if s < len(units):
                zs[s] = scores(s)
            if 0 <= s - ATT_SKEW < len(units):
                mids[s - ATT_SKEW] = costs(s - ATT_SKEW, zs.pop(s - ATT_SKEW))
            if 0 <= s - 2 * ATT_SKEW < len(units):
                weights(s - 2 * ATT_SKEW, *mids.pop(s - 2 * ATT_SKEW))

    diag = ((qi + 1) * tq - 1) // tk
    step([diag], True)
    n_single = diag % ATT_UNROLL

    def single(n, c):
        step([diag - 1 - n], False)
        return c
    lax.fori_loop(0, n_single, single, 0)

    def group(n, c):
        top = diag - 1 - n_single - n * ATT_UNROLL
        step([top - kk for kk in range(ATT_UNROLL)], False)
        return c
    lax.fori_loop(0, diag // ATT_UNROLL, group, 0)

    for t in range(n_tiles):
        o_ref[:, t * LANES:(t + 1) * LANES] = acc_ref[t].astype(BF16)


def _attn(q, k, v, tri, cast_ws=(), l_cast=0):
    s, w = q.shape
    tq, tk = ATT_TQ, ATT_TK
    nt, hpt = w // LANES, HEADS_PER_TILE
    cast_in, cast_out, cast_shapes = _cast_specs(cast_ws, l_cast, s // tq)
    resident = dict(pipeline_mode=pl.Buffered(1))
    return pl.pallas_call(
        functools.partial(_attn_kernel, n_cast=len(cast_ws)),
        grid=(s // tq,),
        in_specs=[
            pl.BlockSpec((tq, w), lambda i: (i, 0)),
            pl.BlockSpec((s, w), lambda i: (0, 0), **resident),
            pl.BlockSpec((s, w), lambda i: (0, 0), **resident),
            pl.BlockSpec((tk, tk + LANES), lambda i: (0, 0), **resident),
        ] + cast_in,
        out_specs=[pl.BlockSpec((tq, w), lambda i: (i, 0))] + cast_out,
        out_shape=[jax.ShapeDtypeStruct((s, w), BF16)] + cast_shapes,
        scratch_shapes=[
            pltpu.VMEM((nt, hpt * tq, LANES), BF16),
            pltpu.VMEM((nt, tq, LANES), F32),
            pltpu.VMEM((nt, hpt * tq, LANES), F32),
            pltpu.VMEM((ATT_UNROLL * nt, hpt * tq, tk), F32),
            pltpu.VMEM((ATT_UNROLL * nt, hpt * tq, tk), BF16),
            pltpu.VMEM((ATT_UNROLL * nt, hpt * tq, tk), BF16),
        ],
        compiler_params=pltpu.CompilerParams(
            dimension_semantics=("arbitrary",), vmem_limit_bytes=VMEM_LIMIT),
        name="attn",
    )(q, k, v, tri, *cast_ws)


def _gate_kernel(h_ref, ya_ref, yb_ref, yc_ref, yd_ref, wg_ref, wb_ref, o_ref, acc_ref):
    b = pl.program_id(2)

    @pl.when((pl.program_id(0) == 0) & (pl.program_id(1) == 0) & (b == 0))
    def _():
        acc_ref[...] = jnp.zeros_like(acc_ref)

    wg = wg_ref[...].astype(BF16)
    wb = wb_ref[...].astype(BF16)
    for c in range(h_ref.shape[0] // GATE_RC):
        rows = pl.ds(c * GATE_RC, GATE_RC)
        gate = jax.nn.sigmoid(jnp.dot(h_ref[rows, :], wg, preferred_element_type=F32))
        y = yd_ref[rows, :]
        for n, y_ref in reversed(list(enumerate((ya_ref, yb_ref, yc_ref)))):
            y = jnp.where(b == n, y_ref[rows, :], y)
        term = gate * jnp.dot(y, wb, preferred_element_type=F32)
        acc = jnp.where(b == 0, term, acc_ref[rows, :] + term)
        acc_ref[rows, :] = acc
        o_ref[rows, :] = acc.astype(BF16)


def _gate(h, ys, w_gate, w_branch, l):
    s, d = h.shape
    w = WIDTH
    tm, tn = GATE_TM, GATE_TN
    nn = d // tn
    y_spec = pl.BlockSpec((tm, w), lambda i, n, b: (i, 0))
    return pl.pallas_call(
        _gate_kernel,
        grid=(s // tm, nn, N_BRANCH),
        in_specs=[
            pl.BlockSpec((tm, d), lambda i, n, b: (i, 0)),
            y_spec, y_spec, y_spec, y_spec,
            pl.BlockSpec((None, d, tn), lambda i, n, b: (l, 0, b * nn + n)),
            pl.BlockSpec((None, None, w, tn), lambda i, n, b: (l, b, 0, n)),
        ],
        out_specs=pl.BlockSpec((tm, tn), lambda i, n, b: (i, n)),
        out_shape=jax.ShapeDtypeStruct((s, d), BF16),
        scratch_shapes=[pltpu.VMEM((tm, tn), F32)],
        compiler_params=pltpu.CompilerParams(
            dimension_semantics=("arbitrary", "arbitrary", "arbitrary"),
            vmem_limit_bytes=VMEM_LIMIT),
        name="gate",
    )(h, *ys, w_gate, w_branch)


def _proj_kernel(x_ref, m_ref, w_ref, o_ref):
    o_ref[...] = x_ref[...] + jnp.dot(m_ref[...], w_ref[...].astype(BF16),
                                      preferred_element_type=F32)


def _proj(x, merged, w_out, l):
    s, d = x.shape
    tm = PROJ_TM
    return pl.pallas_call(
        _proj_kernel,
        grid=(s // tm,),
        in_specs=[
            pl.BlockSpec((tm, d), lambda i: (i, 0)),
            pl.BlockSpec((tm, d), lambda i: (i, 0)),
            pl.BlockSpec((None, d, d), lambda i: (l, 0, 0), pipeline_mode=pl.Buffered(1)),
        ],
        out_specs=pl.BlockSpec((tm, d), lambda i: (i, 0)),
        out_shape=jax.ShapeDtypeStruct((s, d), F32),
        compiler_params=pltpu.CompilerParams(
            dimension_semantics=("arbitrary",), vmem_limit_bytes=VMEM_LIMIT),
        name="proj",
    )(x, merged, w_out)


def _constants():
    j = lax.broadcasted_iota(jnp.int32, (ATT_TK, ATT_TK + LANES), 0)
    c = lax.broadcasted_iota(jnp.int32, (ATT_TK, ATT_TK + LANES), 1)
    tri = ((j > c) | (c >= ATT_TK)).astype(BF16)
    hr = lax.broadcasted_iota(jnp.int32, (WIDTH, WIDTH), 0) // HEAD_DIM
    hc = lax.broadcasted_iota(jnp.int32, (WIDTH, WIDTH), 1) // HEAD_DIM
    blockdiag = (hr == hc).astype(BF16)
    return tri, blockdiag


def kernel(x, ffn1_norm, ffn1_w1, ffn1_w3, ffn1_w2, mix_norm, w_in, conv_a, conv_b, ln_b_gain,
           ln_b_bias, pool_map, pool_scale, q_norm, k_norm, w_branch, w_gate, w_out, ffn2_norm,
           ffn2_w1, ffn2_w3, ffn2_w2):
    b, s, d = x.shape
    depth = w_in.shape[0]
    tri, blockdiag = _constants()
    bf = lambda a: a.astype(BF16)
    row = lambda a: a.reshape(depth, 1, -1)
    pmap = bf(pool_map)
    f1g, f2g, mg = row(ffn1_norm), row(ffn2_norm), row(mix_norm)
    lng, lnb, psc = row(ln_b_gain), row(ln_b_bias), row(pool_scale)
    qg, kg = row(jnp.tile(q_norm, (1, HEADS))), row(jnp.tile(k_norm, (1, HEADS)))
    ffn2_ws = (ffn2_w1, ffn2_w3, ffn2_w2)
    next_ws = (ffn1_w1, ffn1_w3, ffn1_w2, w_in)
    outs = []
    for bi in range(b):
        xb = x[bi]
        *f1, win = [bf(wt[0]) for wt in next_ws]
        for l in range(depth):
            xb = _ffn(xb, f1g, l, *f1)
            h, ya, yb, yc, q, k, v, *f2 = _mix(xb, mg, win, conv_a, conv_b, lng, lnb, pmap, psc,
                                               qg, kg, blockdiag, l, ffn2_ws)
            if l + 1 < depth:
                yd, *f1, win = _attn(q, k, v, tri, next_ws, l + 1)
            else:
                yd, = _attn(q, k, v, tri)
            merged = _gate(h, (ya, yb, yc, yd), w_gate, w_branch, l)
            xb = _proj(xb, merged, w_out, l)
            xb = _ffn(xb, f2g, l, *f2)
        outs.append(xb)
    return jnp.stack(outs, axis=0)
```
